```python
import math
import jax, jax.numpy as jnp
from jax import lax
import numpy as np

D_MODEL = 1024
BATCH = 8
SEQ = 8192
DEPTH = 2
DEC_BATCH = 16
DEC_SEQ = 4096
PAST_LEN = 128

ATTN_WIDTH = D_MODEL // 2
SSM_WIDTH = D_MODEL - ATTN_WIDTH
ATT_HEAD_DIM = 64
ATT_V_DIM = 2 * ATT_HEAD_DIM
ATT_HEADS = ATTN_WIDTH // ATT_V_DIM
QK_WIDTH = ATT_HEADS * 2 * ATT_HEAD_DIM
IN_WIDTH = 2 * QK_WIDTH + ATTN_WIDTH + SSM_WIDTH
SSM_GROUP = 16
SSM_GROUPS = SSM_WIDTH // SSM_GROUP
SSM_STATE = 64
FNET_GROUPS = 4
FNET_GROUP_WIDTH = D_MODEL // FNET_GROUPS
N_EXPERTS = 16
EC_CAPACITY_FACTOR = 2
D_FF = 2816
ROPE_THETA = 10000.0
LN_EPS = 1e-5
Q_BLOCK = 128
N_EVEN = (DEPTH + 1) // 2
N_ODD = DEPTH // 2
DEEPNORM_ALPHA = (2 * DEPTH) ** 0.25
DEEPNORM_BETA = (8 * DEPTH) ** -0.25

kernel_name = 'hybrid_diffattn_s5_fnet_ec_encoder'

F32 = jnp.float32


def layer_norm(x, g, b):
    xf = x.astype(F32)
    mu = jnp.mean(xf, -1, keepdims=True)
    var = jnp.mean(jnp.square(xf - mu), -1, keepdims=True)
    y = (xf - mu) * lax.rsqrt(var + LN_EPS)
    return (y * g.astype(F32) + b.astype(F32)).astype(x.dtype)


def rope_tables(seq_len, dim):
    inv = 1.0 / (ROPE_THETA ** (jnp.arange(0, dim, 2, dtype=F32) / dim))
    ang = jnp.arange(seq_len, dtype=F32)[:, None] * inv[None, :]
    ang = jnp.concatenate([ang, ang], -1)
    return jnp.cos(ang), jnp.sin(ang)


def apply_rope(x, cos, sin):
    half = x.shape[-1] // 2
    rot = jnp.concatenate([-x[..., half:], x[..., :half]], -1)
    c = cos[None, :, None, None, :].astype(x.dtype)
    s = sin[None, :, None, None, :].astype(x.dtype)
    return x * c + rot * s


def diff_attention(q, k, v, lam):
    B, S, H, _, dh = q.shape
    nblk = S // Q_BLOCK
    scale = dh ** -0.5
    qb = q.reshape(B, nblk, Q_BLOCK, H, 2, dh).transpose(1, 0, 2, 3, 4, 5)
    k1 = k[:, :, :, 0]
    k2 = k[:, :, :, 1]

    def block(qblk):
        s1 = jnp.einsum('bqhd,bkhd->bhqk', qblk[:, :, :, 0], k1).astype(F32) * scale
        s2 = jnp.einsum('bqhd,bkhd->bhqk', qblk[:, :, :, 1], k2).astype(F32) * scale
        p = jax.nn.softmax(s1, axis=-1) - lam * jax.nn.softmax(s2, axis=-1)
        return jnp.einsum('bhqk,bkhe->bqhe', p.astype(v.dtype), v)

    out = lax.map(block, qb)
    return out.transpose(1, 0, 2, 3, 4).reshape(B, S, H, v.shape[-1])


def s5_discretize(a_re, a_im, log_dt, b_re, b_im):
    dt = jnp.exp(log_dt)[:, None]
    mag = jnp.exp(a_re * dt)
    lr = mag * jnp.cos(a_im * dt)
    li = mag * jnp.sin(a_im * dt)
    nr = lr - 1.0
    den = a_re * a_re + a_im * a_im
    cr = (nr * a_re + li * a_im) / den
    ci = (li * a_re - nr * a_im) / den
    bbr = cr[..., None] * b_re - ci[..., None] * b_im
    bbi = cr[..., None] * b_im + ci[..., None] * b_re
    return lr, li, bbr, bbi


def _complex_linear_combine(e1, e2):
    a1r, a1i, b1r, b1i = e1
    a2r, a2i, b2r, b2i = e2
    return (a2r * a1r - a2i * a1i,
            a2r * a1i + a2i * a1r,
            a2r * b1r - a2i * b1i + b2r,
            a2r * b1i + a2i * b1r + b2i)


def s5_direction(u, a_re, a_im, log_dt, b_re, b_im, c_re, c_im, reverse):
    lr, li, bbr, bbi = s5_discretize(a_re, a_im, log_dt, b_re, b_im)
    bur = jnp.einsum('bsgh,gph->bsgp', u, bbr)
    bui = jnp.einsum('bsgh,gph->bsgp', u, bbi)
    ar = jnp.broadcast_to(lr, bur.shape)
    ai = jnp.broadcast_to(li, bur.shape)
    _, _, hr, hi = lax.associative_scan(_complex_linear_combine, (ar, ai, bur, bui), reverse=reverse, axis=1)
    return jnp.einsum('bsgp,ghp->bsgh', hr, c_re) - jnp.einsum('bsgp,ghp->bsgh', hi, c_im)


def s5_mixer(u, a_re, a_im, log_dt, b_re, b_im, c_re, c_im, d, glu_w, glu_b):
    B, S, _ = u.shape
    uf = u.astype(F32).reshape(B, S, SSM_GROUPS, SSM_GROUP)
    a_re = a_re.astype(F32); a_im = a_im.astype(F32); log_dt = log_dt.astype(F32)
    b_re = b_re.astype(F32); b_im = b_im.astype(F32); c_re = c_re.astype(F32); c_im = c_im.astype(F32)
    y = (s5_direction(uf, a_re[0], a_im[0], log_dt[0], b_re[0], b_im[0], c_re[0], c_im[0], False)
         + s5_direction(uf, a_re[1], a_im[1], log_dt[1], b_re[1], b_im[1], c_re[1], c_im[1], True)
         + d.astype(F32) * uf)
    y = jax.nn.gelu(y.reshape(B, S, SSM_WIDTH))
    gate = jax.nn.sigmoid(y @ glu_w.astype(F32) + glu_b.astype(F32))
    return (y * gate).astype(u.dtype)


def even_mixer(x, w_in, lam_q1, lam_k1, lam_q2, lam_k2, subln_g,
               s5_a_re, s5_a_im, s5_log_dt, s5_b_re, s5_b_im, s5_c_re, s5_c_im, s5_d,
               s5_glu_w, s5_glu_b, w_out, lambda_init, cos, sin):
    B, S, _ = x.shape
    h = x @ w_in
    q = h[..., :QK_WIDTH].reshape(B, S, ATT_HEADS, 2, ATT_HEAD_DIM)
    k = h[..., QK_WIDTH:2 * QK_WIDTH].reshape(B, S, ATT_HEADS, 2, ATT_HEAD_DIM)
    v = h[..., 2 * QK_WIDTH:2 * QK_WIDTH + ATTN_WIDTH].reshape(B, S, ATT_HEADS, ATT_V_DIM)
    u = h[..., 2 * QK_WIDTH + ATTN_WIDTH:]
    q = apply_rope(q, cos, sin)
    k = apply_rope(k, cos, sin)
    lam = (jnp.exp(jnp.sum(lam_q1.astype(F32) * lam_k1.astype(F32)))
           - jnp.exp(jnp.sum(lam_q2.astype(F32) * lam_k2.astype(F32))) + lambda_init)
    o = diff_attention(q, k, v, lam).astype(F32)
    o = o * lax.rsqrt(jnp.mean(o * o, -1, keepdims=True) + LN_EPS) * subln_g.astype(F32)
    attn = (o * (1.0 - lambda_init)).reshape(B, S, ATTN_WIDTH).astype(x.dtype)
    ssm = s5_mixer(u, s5_a_re, s5_a_im, s5_log_dt, s5_b_re, s5_b_im, s5_c_re, s5_c_im, s5_d, s5_glu_w, s5_glu_b)
    return jnp.concatenate([attn, ssm], -1) @ w_out


def fourier_mixer(x, w_out):
    B, S, _ = x.shape
    xg = x.astype(F32).reshape(B, S, FNET_GROUPS, FNET_GROUP_WIDTH)
    f = jnp.fft.fft2(xg, axes=(1, 3), norm='ortho').real
    return f.reshape(B, S, D_MODEL).astype(x.dtype) @ w_out


def expert_choice_ffn(x, w_router, w1, w3, w2):
    B, S, D = x.shape
    n = B * S
    cap = max(1, EC_CAPACITY_FACTOR * n // N_EXPERTS)
    xt = x.reshape(n, D)
    aff = jax.nn.softmax(xt.astype(F32) @ w_router.astype(F32), axis=-1)
    gate, idx = lax.top_k(aff.T, cap)
    xs = xt[idx]

    def expert(args):
        xe, a, c, o = args
        return (jax.nn.silu(xe @ a) * (xe @ c)) @ o

    ye = lax.map(expert, (xs, w1, w3, w2)) * gate[..., None].astype(x.dtype)
    y = jnp.zeros_like(xt).at[idx.reshape(-1)].add(ye.reshape(-1, D))
    return y.reshape(B, S, D)


def trunk(x, w_in, lam_q1, lam_k1, lam_q2, lam_k2, subln_g,
          s5_a_re, s5_a_im, s5_log_dt, s5_b_re, s5_b_im, s5_c_re, s5_c_im, s5_d, s5_glu_w, s5_glu_b,
          w_out_even, w_out_odd, ln_mix_g, ln_mix_b, w_router, w_ff1, w_ff3, w_ff2, ln_ffn_g, ln_ffn_b):
    cos, sin = rope_tables(x.shape[1], ATT_HEAD_DIM)
    for l in range(DEPTH):
        j = l // 2
        if l % 2 == 0:
            lambda_init = 0.8 - 0.6 * math.exp(-0.3 * l)
            m = even_mixer(x, w_in[j], lam_q1[j], lam_k1[j], lam_q2[j], lam_k2[j], subln_g[j],
                           s5_a_re[j], s5_a_im[j], s5_log_dt[j], s5_b_re[j], s5_b_im[j], s5_c_re[j], s5_c_im[j],
                           s5_d[j], s5_glu_w[j], s5_glu_b[j], w_out_even[j], lambda_init, cos, sin)
        else:
            m = fourier_mixer(x, w_out_odd[j])
        x = layer_norm(DEEPNORM_ALPHA * x + m, ln_mix_g[l], ln_mix_b[l])
        f = expert_choice_ffn(x, w_router[l], w_ff1[l], w_ff3[l], w_ff2[l])
        x = layer_norm(DEEPNORM_ALPHA * x + f, ln_ffn_g[l], ln_ffn_b[l])
    return x


def setup_inputs(seed: int = 0) -> dict:
    key = jax.random.key(seed)
    ks = jax.random.split(key, 32)
    nrm = lambda k, shape, s: jax.random.normal(k, shape, F32) * s
    G, P, H = SSM_GROUPS, SSM_STATE, SSM_GROUP
    a_im0 = jnp.pi * jnp.arange(P, dtype=F32)
    return {
        'x_prompt': nrm(ks[0], (BATCH, SEQ, D_MODEL), 1.0),
        'x_sample': nrm(ks[1], (DEC_BATCH, DEC_SEQ, D_MODEL), 1.0),
        'w_in': nrm(ks[2], (N_EVEN, D_MODEL, IN_WIDTH), D_MODEL ** -0.5),
        'lam_q1': nrm(ks[3], (N_EVEN, ATT_HEAD_DIM), 0.1),
        'lam_k1': nrm(ks[4], (N_EVEN, ATT_HEAD_DIM), 0.1),
        'lam_q2': nrm(ks[5], (N_EVEN, ATT_HEAD_DIM), 0.1),
        'lam_k2': nrm(ks[6], (N_EVEN, ATT_HEAD_DIM), 0.1),
        'subln_g': 1.0 + nrm(ks[7], (N_EVEN, ATT_V_DIM), 0.02),
        's5_a_re': -0.5 + nrm(ks[8], (N_EVEN, 2, G, P), 0.01),
        's5_a_im': a_im0 + nrm(ks[9], (N_EVEN, 2, G, P), 0.01),
        's5_log_dt': jax.random.uniform(ks[10], (N_EVEN, 2, G), F32, math.log(1e-3), math.log(1e-1)),
        's5_b_re': nrm(ks[11], (N_EVEN, 2, G, P, H), (2 * H) ** -0.5),
        's5_b_im': nrm(ks[12], (N_EVEN, 2, G, P, H), (2 * H) ** -0.5),
        's5_c_re': nrm(ks[13], (N_EVEN, 2, G, H, P), P ** -0.5),
        's5_c_im': nrm(ks[14], (N_EVEN, 2, G, H, P), P ** -0.5),
        's5_d': nrm(ks[15], (N_EVEN, G, H), 1.0),
        's5_glu_w': nrm(ks[16], (N_EVEN, SSM_WIDTH, SSM_WIDTH), SSM_WIDTH ** -0.5),
        's5_glu_b': nrm(ks[17], (N_EVEN, SSM_WIDTH), 0.02),
        'w_out_even': nrm(ks[18], (N_EVEN, D_MODEL, D_MODEL), D_MODEL ** -0.5 * DEEPNORM_BETA),
        'w_out_odd': nrm(ks[19], (N_ODD, D_MODEL, D_MODEL), D_MODEL ** -0.5 * DEEPNORM_BETA),
        'ln_mix_g': 1.0 + nrm(ks[20], (DEPTH, D_MODEL), 0.02),
        'ln_mix_b': nrm(ks[21], (DEPTH, D_MODEL), 0.02),
        'w_router': nrm(ks[22], (DEPTH, D_MODEL, N_EXPERTS), D_MODEL ** -0.5),
        'w_ff1': nrm(ks[23], (DEPTH, N_EXPERTS, D_MODEL, D_FF), D_MODEL ** -0.5),
        'w_ff3': nrm(ks[24], (DEPTH, N_EXPERTS, D_MODEL, D_FF), D_MODEL ** -0.5),
        'w_ff2': nrm(ks[25], (DEPTH, N_EXPERTS, D_FF, D_MODEL), D_FF ** -0.5 * DEEPNORM_BETA),
        'ln_ffn_g': 1.0 + nrm(ks[26], (DEPTH, D_MODEL), 0.02),
        'ln_ffn_b': nrm(ks[27], (DEPTH, D_MODEL), 0.02),
    }


def reference(x_prompt, x_sample, w_in, lam_q1, lam_k1, lam_q2, lam_k2, subln_g,
              s5_a_re, s5_a_im, s5_log_dt, s5_b_re, s5_b_im, s5_c_re, s5_c_im, s5_d, s5_glu_w, s5_glu_b,
              w_out_even, w_out_odd, ln_mix_g, ln_mix_b, w_router, w_ff1, w_ff3, w_ff2, ln_ffn_g, ln_ffn_b):
    y_prompt = trunk(x_prompt, w_in, lam_q1, lam_k1, lam_q2, lam_k2, subln_g,
                     s5_a_re, s5_a_im, s5_log_dt, s5_b_re, s5_b_im, s5_c_re, s5_c_im, s5_d, s5_glu_w, s5_glu_b,
                     w_out_even, w_out_odd, ln_mix_g, ln_mix_b, w_router, w_ff1, w_ff3, w_ff2, ln_ffn_g, ln_ffn_b)
    y_sample = trunk(x_sample, w_in, lam_q1, lam_k1, lam_q2, lam_k2, subln_g,
                     s5_a_re, s5_a_im, s5_log_dt, s5_b_re, s5_b_im, s5_c_re, s5_c_im, s5_d, s5_glu_w, s5_glu_b,
                     w_out_even, w_out_odd, ln_mix_g, ln_mix_b, w_router, w_ff1, w_ff3, w_ff2, ln_ffn_g, ln_ffn_b)
    return (y_prompt, y_sample)
```

```python
import functools
import math

import jax
import jax.numpy as jnp
from jax import lax
from jax.experimental import pallas as pl
from jax.experimental.pallas import tpu as pltpu

F32 = jnp.float32
BF16 = jnp.bfloat16
HI = lax.Precision.HIGHEST

ATT_HEAD_DIM = 64
ATT_V_DIM = 2 * ATT_HEAD_DIM
FNET_GROUPS = 4
EC_CAPACITY_FACTOR = 2
ROPE_THETA = 10000.0
LN_EPS = 1e-5
DEPTH = 2
DEEPNORM_ALPHA = (2 * DEPTH) ** 0.25

V7X_VMEM_BYTES = 64 * 1024 * 1024
V7X_LANES = 128

S5_CHUNK = 64


def _cparams(semantics, vmem_mib):
    return pltpu.CompilerParams(dimension_semantics=semantics,
                                vmem_limit_bytes=min(vmem_mib * 1024 * 1024, V7X_VMEM_BYTES * 7 // 8))


def _layer_norm(z, g, b):
    mu = jnp.mean(z, -1, keepdims=True)
    zc = z - mu
    var = jnp.mean(zc * zc, -1, keepdims=True)
    return zc * lax.rsqrt(var + LN_EPS) * g + b


def _proj_in_kernel(x_ref, w_ref, wrot_ref, cos_ref, sin_ref, q_ref, k_ref, v_ref, u_ref, *, qk, av, scale):
    xb = x_ref[...].astype(BF16)
    h = jnp.dot(xb, w_ref[...], preferred_element_type=F32)
    hr = jnp.dot(xb, wrot_ref[...], preferred_element_type=F32)
    roped = h[:, :2 * qk] * cos_ref[...] + hr * sin_ref[...]
    q_ref[...] = (roped[:, :qk] * scale).astype(BF16)
    k_ref[...] = roped[:, qk:].astype(BF16)
    v_ref[...] = h[:, 2 * qk:2 * qk + av].astype(BF16)
    u_ref[...] = h[:, 2 * qk + av:]


def _proj_in(x2, w, wrot, cos_t, sin_t, seq, qk, av):
    n, d = x2.shape
    sw = w.shape[1] - 2 * qk - av
    tm = min(512, seq)
    nsb = seq // tm
    kern = functools.partial(_proj_in_kernel, qk=qk, av=av, scale=ATT_HEAD_DIM ** -0.5)
    return pl.pallas_call(
        kern,
        grid=(n // tm,),
        in_specs=[
            pl.BlockSpec((tm, d), lambda i: (i, 0)),
            pl.BlockSpec(w.shape, lambda i: (0, 0)),
            pl.BlockSpec(wrot.shape, lambda i: (0, 0)),
            pl.BlockSpec((tm, 2 * qk), lambda i: (i % nsb, 0)),
            pl.BlockSpec((tm, 2 * qk), lambda i: (i % nsb, 0)),
        ],
        out_specs=[
            pl.BlockSpec((tm, qk), lambda i: (i, 0)),
            pl.BlockSpec((tm, qk), lambda i: (i, 0)),
            pl.BlockSpec((tm, av), lambda i: (i, 0)),
            pl.BlockSpec((tm, sw), lambda i: (i, 0)),
        ],
        out_shape=[
            jax.ShapeDtypeStruct((n, qk), BF16),
            jax.ShapeDtypeStruct((n, qk), BF16),
            jax.ShapeDtypeStruct((n, av), BF16),
            jax.ShapeDtypeStruct((n, sw), F32),
        ],
        compiler_params=_cparams(("parallel",), 48),
        name="proj_in",
    )(x2, w, wrot, cos_t, sin_t)


def _attn_kernel(lam_ref, q_ref, k_ref, vt_ref, g_ref, o_ref, acc1, acc2, *, tk, nkv, out_scale):
    q = q_ref[...]
    tq = q.shape[0]
    lane = lax.broadcasted_iota(jnp.int32, q.shape, 1)
    zero = jnp.zeros_like(q)
    q1 = jnp.where(lane < ATT_HEAD_DIM, q, zero)
    q2 = jnp.where(lane >= ATT_HEAD_DIM, q, zero)
    acc1[...] = jnp.zeros_like(acc1)
    acc2[...] = jnp.zeros_like(acc2)

    def stream(kb, vb, qz, m, l, acc):
        s = lax.dot_general(kb, qz, (((1,), (1,)), ((), ())), preferred_element_type=F32)
        mn = jnp.maximum(m, jnp.max(s, axis=0, keepdims=True))
        p = jnp.exp(s - mn)
        a = jnp.exp(m - mn)
        l = a * l + jnp.sum(p, axis=0, keepdims=True)
        acc[...] = a * acc[...] + jnp.dot(vb, p.astype(BF16), preferred_element_type=F32)
        return mn, l

    def body(j, carry):
        m1, l1, m2, l2 = carry
        kb = k_ref[pl.ds(pl.multiple_of(j * tk, tk), tk), :]
        vb = vt_ref[j]
        m1, l1 = stream(kb, vb, q1, m1, l1, acc1)
        m2, l2 = stream(kb, vb, q2, m2, l2, acc2)
        return m1, l1, m2, l2

    neg = jnp.full((1, tq), -1e30, F32)
    zer = jnp.zeros((1, tq), F32)
    _, l1, _, l2 = lax.fori_loop(0, nkv, body, (neg, zer, neg, zer))
    o = acc1[...] / l1 - lam_ref[0] * (acc2[...] / l2)
    ms = jnp.mean(o * o, axis=0, keepdims=True)
    o = o * lax.rsqrt(ms + LN_EPS) * g_ref[...] * out_scale
    o_ref[...] = o.astype(o_ref.dtype)


def _diff_attention(q, k, vt, lam, g_col, lambda_init, tk):
    b, s, _ = q.shape
    heads = vt.shape[1]
    nkv = vt.shape[2]
    dv = vt.shape[3]
    tq = min(512, s)
    kern = functools.partial(_attn_kernel, tk=tk, nkv=nkv, out_scale=1.0 - lambda_init)
    return pl.pallas_call(
        kern,
        grid=(b, heads, s // tq),
        in_specs=[
            pl.BlockSpec(memory_space=pltpu.SMEM),
            pl.BlockSpec((None, tq, 2 * ATT_HEAD_DIM), lambda bi, hi, qi: (bi, qi, hi)),
            pl.BlockSpec((None, s, 2 * ATT_HEAD_DIM), lambda bi, hi, qi: (bi, 0, hi)),
            pl.BlockSpec((None, None, nkv, dv, tk), lambda bi, hi, qi: (bi, hi, 0, 0, 0)),
            pl.BlockSpec((dv, 1), lambda bi, hi, qi: (0, 0)),
        ],
        out_specs=pl.BlockSpec((None, None, dv, tq), lambda bi, hi, qi: (bi, hi, 0, qi)),
        out_shape=jax.ShapeDtypeStruct((b, heads, dv, s), BF16),
        scratch_shapes=[pltpu.VMEM((dv, tq), F32), pltpu.VMEM((dv, tq), F32)],
        compiler_params=_cparams(("parallel", "parallel", "arbitrary"), 48),
        name="diff_attn",
    )(lam, q, k, vt, g_col)


def _s5_kernel(u_ref, t_ref, we_ref, ws_ref, a1_ref, a2_ref, y_ref, *, nc, nsteps, p2):
    u = u_ref[...]
    r = u.shape[0]
    e = jnp.dot(u, we_ref[...], preferred_element_type=F32)
    c = lax.broadcasted_iota(jnp.int32, (r, p2), 0) % nc
    half = p2 // 2
    hf = e[:, :p2]
    hb = e[:, p2:]
    for kk in range(nsteps):
        sh = 1 << kk
        pf = jnp.where(c >= sh, pltpu.roll(hf, sh, axis=0), 0.0)
        hf = hf + a1_ref[kk:kk + 1, :p2] * pf + a2_ref[kk:kk + 1, :p2] * pltpu.roll(pf, half, axis=1)
        pb = jnp.where(c < nc - sh, pltpu.roll(hb, r - sh, axis=0), 0.0)
        hb = hb + a1_ref[kk:kk + 1, p2:] * pb + a2_ref[kk:kk + 1, p2:] * pltpu.roll(pb, half, axis=1)
    hf_in = jnp.where(c >= 1, pltpu.roll(hf, 1, axis=0), 0.0)
    hb_in = jnp.where(c < nc - 1, pltpu.roll(hb, r - 1, axis=0), 0.0)
    hin = jnp.concatenate([hf_in, hb_in], axis=1).astype(BF16)
    y = jnp.dot(u, t_ref[...], preferred_element_type=F32)
    y = y + jnp.dot(hin, ws_ref[...], preferred_element_type=F32)
    y_ref[...] = jax.nn.gelu(y)


def _s5_scan(ug, tabs, nc):
    t_tot, we, ws, a1, a2 = tabs
    g, r, lh = ug.shape
    p4 = we.shape[2]
    nsteps = a1.shape[1]
    kern = functools.partial(_s5_kernel, nc=nc, nsteps=nsteps, p2=p4 // 2)
    return pl.pallas_call(
        kern,
        grid=(g,),
        in_specs=[
            pl.BlockSpec((None, r, lh), lambda i: (i, 0, 0)),
            pl.BlockSpec((None, lh, lh), lambda i: (i, 0, 0)),
            pl.BlockSpec((None, lh, p4), lambda i: (i, 0, 0)),
            pl.BlockSpec((None, p4, lh), lambda i: (i, 0, 0)),
            pl.BlockSpec((None, nsteps, p4), lambda i: (i, 0, 0)),
            pl.BlockSpec((None, nsteps, p4), lambda i: (i, 0, 0)),
        ],
        out_specs=pl.BlockSpec((None, r, lh), lambda i: (i, 0, 0)),
        out_shape=jax.ShapeDtypeStruct((g, r, lh), F32),
        compiler_params=_cparams(("parallel",), 48),
        name="s5_scan",
    )(ug, t_tot, we, ws, a1, a2)


def _s5_tables(a_re, a_im, log_dt, b_re, b_im, c_re, c_im, d, chunk, nsteps):
    L = chunk
    G, P = a_re.shape[1], a_re.shape[2]
    H = b_re.shape[3]
    dt = jnp.exp(log_dt)[..., None]
    mag = jnp.exp(a_re * dt)
    lr = mag * jnp.cos(a_im * dt)
    li = mag * jnp.sin(a_im * dt)
    nr = lr - 1.0
    den = a_re * a_re + a_im * a_im
    cr = (nr * a_re + li * a_im) / den
    ci = (li * a_re - nr * a_im) / den
    bbr = cr[..., None] * b_re - ci[..., None] * b_im
    bbi = cr[..., None] * b_im + ci[..., None] * b_re
    j = jnp.arange(L + 1, dtype=F32)[:, None, None, None]
    pm = jnp.exp(j * (a_re * dt)[None])
    ang = j * (a_im * dt)[None]
    pr = pm * jnp.cos(ang)
    pi = pm * jnp.sin(ang)
    cbr = c_re[..., None] * bbr[:, :, None] - c_im[..., None] * bbi[:, :, None]
    cbi = c_re[..., None] * bbi[:, :, None] + c_im[..., None] * bbr[:, :, None]
    kern = (jnp.einsum('jdgp,dghpk->dgjhk', pr[:L], cbr, precision=HI)
            - jnp.einsum('jdgp,dghpk->dgjhk', pi[:L], cbi, precision=HI))
    s_i = jnp.arange(L)[:, None]
    t_i = jnp.arange(L)[None, :]
    lag_f = jnp.clip(t_i - s_i, 0, L - 1)
    lag_b = jnp.clip(s_i - t_i, 0, L - 1)
    tf = kern[0][:, lag_f] * (t_i >= s_i)[None, :, :, None, None]
    tb = kern[1][:, lag_b] * (s_i >= t_i)[None, :, :, None, None]
    skip = (jnp.eye(L, dtype=F32)[None, :, :, None, None] * jnp.eye(H, dtype=F32)[None, None, None]
            * d[:, None, None, :, None])
    t_tot = (tf + tb + skip).transpose(0, 1, 4, 2, 3).reshape(G, L * H, L * H)
    def state_in(pw_r, pw_i, br, bi):
        re = pw_r[..., None] * br[None] - pw_i[..., None] * bi[None]
        im = pw_r[..., None] * bi[None] + pw_i[..., None] * br[None]
        both = jnp.concatenate([re, im], axis=2)
        return both.transpose(1, 0, 3, 2).reshape(G, L * H, 2 * P)
    we = jnp.concatenate([state_in(pr[:L][::-1, 0], pi[:L][::-1, 0], bbr[0], bbi[0]),
                          state_in(pr[:L, 1], pi[:L, 1], bbr[1], bbi[1])], axis=2)
    def state_out(pw_r, pw_i, cre, cim):
        re = cre[None] * pw_r[:, :, None, :] - cim[None] * pw_i[:, :, None, :]
        im = cre[None] * pw_i[:, :, None, :] + cim[None] * pw_r[:, :, None, :]
        both = jnp.concatenate([re, -im], axis=3)
        return both.transpose(1, 3, 0, 2).reshape(G, 2 * P, L * H)
    ws = jnp.concatenate([state_out(pr[1:, 0], pi[1:, 0], c_re[0], c_im[0]),
                          state_out(pr[1:][::-1, 1], pi[1:][::-1, 1], c_re[1], c_im[1])], axis=1)
    ar, ai = pr[L], pi[L]
    a1, a2 = [], []
    for _ in range(nsteps):
        a1.append(jnp.concatenate([ar[0], ar[0], ar[1], ar[1]], axis=-1))
        a2.append(jnp.concatenate([-ai[0], ai[0], -ai[1], ai[1]], axis=-1))
        ar, ai = ar * ar - ai * ai, 2.0 * ar * ai
    a1 = jnp.stack(a1, axis=1)
    a2 = jnp.stack(a2, axis=1)
    return t_tot.astype(BF16), we.astype(BF16), ws.astype(BF16), a1, a2


def _router(x1, wr_ref):
    logits = jnp.dot(x1, wr_ref[...], preferred_element_type=F32, precision=HI)
    logits = logits - jnp.max(logits, axis=-1, keepdims=True)
    ex = jnp.exp(logits)
    return ex / jnp.sum(ex, axis=-1, keepdims=True)


def _mix_even_kernel(x_ref, attn_ref, y_ref, gw_ref, gb_ref, wo_ref, lg_ref, lb_ref, wr_ref,
                     x1_ref, aff_ref, *, aw):
    y = y_ref[...]
    gate = jax.nn.sigmoid(jnp.dot(y.astype(BF16), gw_ref[...], preferred_element_type=F32) + gb_ref[...])
    ssm = (y * gate).astype(BF16)
    m = jnp.dot(attn_ref[...], wo_ref[:aw, :], preferred_element_type=F32)
    m = m + jnp.dot(ssm, wo_ref[aw:, :], preferred_element_type=F32)
    x1 = _layer_norm(DEEPNORM_ALPHA * x_ref[...] + m, lg_ref[...], lb_ref[...])
    x1_ref[...] = x1
    aff_ref[...] = _router(x1, wr_ref)


def _mix_odd_kernel(x_ref, f_ref, wo_ref, lg_ref, lb_ref, wr_ref, x1_ref, aff_ref):
    m = jnp.dot(f_ref[...].astype(BF16), wo_ref[...], preferred_element_type=F32)
    x1 = _layer_norm(DEEPNORM_ALPHA * x_ref[...] + m, lg_ref[...], lb_ref[...])
    x1_ref[...] = x1
    aff_ref[...] = _router(x1, wr_ref)


def _full(a):
    nd = a.ndim
    return pl.BlockSpec(a.shape, lambda i: (0,) * nd)


def _rows(tm, width):
    return pl.BlockSpec((tm, width), lambda i: (i, 0))


def _mix_even(x2, attn, yg, gw, gb, wo, lg, lb, wr):
    n, d = x2.shape
    ne = wr.shape[1]
    tm = min(512, n)
    aw = attn.shape[1]
    return pl.pallas_call(
        functools.partial(_mix_even_kernel, aw=aw),
        grid=(n // tm,),
        in_specs=[_rows(tm, d), _rows(tm, aw), _rows(tm, yg.shape[1]), _full(gw), _full(gb), _full(wo),
                  _full(lg), _full(lb), _full(wr)],
        out_specs=[_rows(tm, d), _rows(tm, ne)],
        out_shape=[jax.ShapeDtypeStruct((n, d), F32), jax.ShapeDtypeStruct((n, ne), F32)],
        compiler_params=_cparams(("parallel",), 48),
        name="mix_even",
    )(x2, attn, yg, gw, gb, wo, lg, lb, wr)


def _mix_odd(x2, f, wo, lg, lb, wr):
    n, d = x2.shape
    ne = wr.shape[1]
    tm = min(512, n)
    return pl.pallas_call(
        _mix_odd_kernel,
        grid=(n // tm,),
        in_specs=[_rows(tm, d), _rows(tm, d), _full(wo), _full(lg), _full(lb), _full(wr)],
        out_specs=[_rows(tm, d), _rows(tm, ne)],
        out_shape=[jax.ShapeDtypeStruct((n, d), F32), jax.ShapeDtypeStruct((n, ne), F32)],
        compiler_params=_cparams(("parallel",), 48),
        name="mix_odd",
    )(x2, f, wo, lg, lb, wr)


def _chan_dft_kernel(x_ref, w_ref, z_ref, *, groups, gw):
    x = x_ref[...].astype(BF16)
    w = w_ref[...]
    for gi in range(groups):
        z = jnp.dot(x[:, gi * gw:(gi + 1) * gw], w, preferred_element_type=F32)
        z_ref[0, :, gi * gw:(gi + 1) * gw] = z[:, :gw].astype(BF16)
        z_ref[1, :, gi * gw:(gi + 1) * gw] = z[:, gw:].astype(BF16)


def _chan_dft(x3, w):
    b, s, d = x3.shape
    tm = min(512, s)
    gw = d // FNET_GROUPS
    return pl.pallas_call(
        functools.partial(_chan_dft_kernel, groups=FNET_GROUPS, gw=gw),
        grid=(b, s // tm),
        in_specs=[pl.BlockSpec((None, tm, d), lambda bi, i: (bi, i, 0)),
                  pl.BlockSpec(w.shape, lambda bi, i: (0, 0))],
        out_specs=pl.BlockSpec((None, 2, tm, d), lambda bi, i: (bi, 0, i, 0)),
        out_shape=jax.ShapeDtypeStruct((b, 2, s, d), BF16),
        compiler_params=_cparams(("parallel", "parallel"), 32),
        name="chan_dft",
    )(x3, w)


def _seq_dft_kernel(t_ref, z_ref, o_ref, acc_ref, *, scale):
    kk = pl.program_id(2)

    @pl.when(kk == 0)
    def _():
        acc_ref[...] = jnp.zeros_like(acc_ref)

    acc_ref[...] += jnp.dot(t_ref[...], z_ref[...], preferred_element_type=F32)

    @pl.when(kk == pl.num_programs(2) - 1)
    def _():
        o_ref[...] = acc_ref[...] * scale


def _seq_dft(trig, z, scale):
    b, s2, d = z.shape
    s = s2 // 2
    tm = min(1024, s)
    tk = min(1024, s2)
    return pl.pallas_call(
        functools.partial(_seq_dft_kernel, scale=scale),
        grid=(b, s // tm, s2 // tk),
        in_specs=[pl.BlockSpec((tm, tk), lambda bi, i, kk: (i, kk)),
                  pl.BlockSpec((None, tk, d), lambda bi, i, kk: (bi, kk, 0))],
        out_specs=pl.BlockSpec((None, tm, d), lambda bi, i, kk: (bi, i, 0)),
        out_shape=jax.ShapeDtypeStruct((b, s, d), F32),
        scratch_shapes=[pltpu.VMEM((tm, d), F32)],
        compiler_params=_cparams(("parallel", "parallel", "arbitrary"), 48),
        name="seq_dft",
    )(trig, z)


def _dft_tables(seq, gw):
    c = jnp.arange(gw, dtype=jnp.int32)
    angc = (2.0 * math.pi / gw) * ((c[:, None] * c[None, :]) % gw).astype(F32)
    wc = jnp.concatenate([jnp.cos(angc), -jnp.sin(angc)], axis=1).astype(BF16)
    k = jnp.arange(seq, dtype=jnp.int32)
    angs = (2.0 * math.pi / seq) * ((k[:, None] * k[None, :]) % seq).astype(F32)
    trig = jnp.concatenate([jnp.cos(angs), jnp.sin(angs)], axis=1).astype(BF16)
    return wc, trig


def _ffn_kernel(idx_cur, idx_nxt, gate_ref, x_hbm, w1_ref, w3_ref, w2_ref, ye_ref, xbuf, sem, *, tm, ffc):
    s = pl.program_id(0)
    ns = pl.num_programs(0)
    slot = s % 2

    def gather(idx_ref, sl):
        def body(r, carry):
            t = idx_ref[0, 0, r]
            pltpu.make_async_copy(x_hbm.at[pl.ds(t, 1), :], xbuf.at[sl, pl.ds(r, 1), :], sem.at[sl]).start()
            return carry
        lax.fori_loop(0, tm, body, 0, unroll=8)

    @pl.when(s == 0)
    def _():
        gather(idx_cur, 0)

    @pl.when(s + 1 < ns)
    def _():
        gather(idx_nxt, 1 - slot)

    pltpu.make_async_copy(x_hbm.at[pl.ds(0, tm), :], xbuf.at[slot], sem.at[slot]).wait()
    xb = xbuf[slot].astype(BF16)
    dff = w1_ref.shape[1]
    acc = jnp.zeros(ye_ref.shape, F32)
    for c0 in range(0, dff, ffc):
        h1 = jnp.dot(xb, w1_ref[:, c0:c0 + ffc], preferred_element_type=F32)
        h3 = jnp.dot(xb, w3_ref[:, c0:c0 + ffc], preferred_element_type=F32)
        gg = (jax.nn.silu(h1) * h3).astype(BF16)
        acc = acc + jnp.dot(gg, w2_ref[c0:c0 + ffc, :], preferred_element_type=F32)
    ye_ref[...] = acc * gate_ref[...]


def _moe_ffn(x1, idx, gate, w1, w3, w2):
    n, d = x1.shape
    e, cap = idx.shape
    dff = w1.shape[2]
    tm = min(256, cap)
    nblk = cap // tm
    ns = e * nblk
    ffc = dff // 2 if (dff // 2) % V7X_LANES == 0 else dff
    idx3 = idx.reshape(ns, 1, tm)
    gate2 = gate.reshape(e * cap, 1)
    smem_blk = lambda f: pl.BlockSpec((1, 1, tm), f, memory_space=pltpu.SMEM)
    return pl.pallas_call(
        functools.partial(_ffn_kernel, tm=tm, ffc=ffc),
        grid=(ns,),
        in_specs=[
            smem_blk(lambda s: (s, 0, 0)),
            smem_blk(lambda s: (jnp.minimum(s + 1, ns - 1), 0, 0)),
            pl.BlockSpec((tm, 1), lambda s: (s, 0)),
            pl.BlockSpec(memory_space=pl.ANY),
            pl.BlockSpec((None, d, dff), lambda s: (s // nblk, 0, 0)),
            pl.BlockSpec((None, d, dff), lambda s: (s // nblk, 0, 0)),
            pl.BlockSpec((None, dff, d), lambda s: (s // nblk, 0, 0)),
        ],
        out_specs=pl.BlockSpec((tm, d), lambda s: (s, 0)),
        out_shape=jax.ShapeDtypeStruct((e * cap, d), F32),
        scratch_shapes=[pltpu.VMEM((2, tm, d), F32), pltpu.SemaphoreType.DMA((2,))],
        compiler_params=_cparams(("arbitrary",), 56),
        name="moe_ffn",
    )(idx3, idx3, gate2, x1, w1, w3, w2)


def _combine_kernel(tile_ref, chunk_ref, first_ref, last_ref, valid_ref,
                    slot_ref, tok_ref, x_ref, ye_hbm, lg_ref, lb_ref, o_ref, acc_ref, buf, sem, *, tt, ca):
    w = pl.program_id(0)

    @pl.when(first_ref[w] == 1)
    def _():
        acc_ref[...] = jnp.zeros_like(acc_ref)

    @pl.when(valid_ref[w] == 1)
    def _():
        def body(r, carry):
            a = slot_ref[0, 0, r]
            pltpu.make_async_copy(ye_hbm.at[pl.ds(a, 1), :], buf.at[pl.ds(r, 1), :], sem.at[0]).start()
            return carry
        lax.fori_loop(0, ca, body, 0, unroll=8)
        pltpu.make_async_copy(ye_hbm.at[pl.ds(0, ca), :], buf, sem.at[0]).wait()
        t0 = tile_ref[w] * tt
        row = lax.broadcasted_iota(jnp.int32, (tt, ca), 0) + t0
        onehot = jnp.where(tok_ref[0] == row, 1.0, 0.0).astype(BF16)
        acc_ref[...] += jnp.dot(onehot, buf[...].astype(BF16), preferred_element_type=F32)

    @pl.when(last_ref[w] == 1)
    def _():
        o_ref[...] = _layer_norm(DEEPNORM_ALPHA * x_ref[...] + acc_ref[...], lg_ref[...], lb_ref[...])


def _moe_combine(x1, ye, idx, lg, lb):
    n, d = x1.shape
    na = ye.shape[0]
    tt = min(256, n)
    ca = min(256, na)
    nt = n // tt
    nch = na // ca
    nw = nt + nch
    tok_flat = idx.reshape(-1)
    order = jnp.argsort(tok_flat).astype(jnp.int32)
    tok_sorted = tok_flat[order]
    bounds = jnp.searchsorted(tok_sorted, jnp.arange(nt + 1, dtype=jnp.int32) * tt, side='left').astype(jnp.int32)
    start, end = bounds[:-1], bounds[1:]
    c_lo = jnp.minimum(start // ca, nch - 1)
    c_hi = jnp.where(end > start, (end - 1) // ca, c_lo)
    cnt = c_hi - c_lo + 1
    off = jnp.cumsum(cnt) - cnt
    total = off[-1] + cnt[-1]
    wi = jnp.arange(nw, dtype=jnp.int32)
    tile_w = jnp.clip(jnp.searchsorted(off, wi, side='right').astype(jnp.int32) - 1, 0, nt - 1)
    valid = wi < total
    rel = wi - off[tile_w]
    chunk_w = jnp.where(valid, c_lo[tile_w] + rel, nch - 1).astype(jnp.int32)
    first = (valid & (rel == 0)).astype(jnp.int32)
    last = (valid & (rel == cnt[tile_w] - 1)).astype(jnp.int32)
    tile_w = jnp.where(valid, tile_w, nt - 1).astype(jnp.int32)
    valid = valid.astype(jnp.int32)
    slot3 = order.reshape(nch, 1, ca)
    tok3 = tok_sorted.reshape(nch, 1, ca)
    grid_spec = pltpu.PrefetchScalarGridSpec(
        num_scalar_prefetch=5,
        grid=(nw,),
        in_specs=[
            pl.BlockSpec((1, 1, ca), lambda w, tl, ch, fi, la, va: (ch[w], 0, 0), memory_space=pltpu.SMEM),
            pl.BlockSpec((1, 1, ca), lambda w, tl, ch, fi, la, va: (ch[w], 0, 0)),
            pl.BlockSpec((tt, d), lambda w, tl, ch, fi, la, va: (tl[w], 0)),
            pl.BlockSpec(memory_space=pl.ANY),
            pl.BlockSpec((1, d), lambda w, tl, ch, fi, la, va: (0, 0)),
            pl.BlockSpec((1, d), lambda w, tl, ch, fi, la, va: (0, 0)),
        ],
        out_specs=pl.BlockSpec((tt, d), lambda w, tl, ch, fi, la, va: (tl[w], 0)),
        scratch_shapes=[pltpu.VMEM((tt, d), F32), pltpu.VMEM((ca, d), F32), pltpu.SemaphoreType.DMA((1,))],
    )
    return pl.pallas_call(
        functools.partial(_combine_kernel, tt=tt, ca=ca),
        grid_spec=grid_spec,
        out_shape=jax.ShapeDtypeStruct((n, d), F32),
        compiler_params=_cparams(("arbitrary",), 32),
        name="moe_combine",
    )(tile_w, chunk_w, first, last, valid, slot3, tok3, x1, ye, lg, lb)


def _expert_choice(x1, aff, w1, w3, w2, lg, lb):
    n = x1.shape[0]
    ne = aff.shape[1]
    cap = max(1, EC_CAPACITY_FACTOR * n // ne)
    gate, idx = lax.top_k(aff.T, cap)
    ye = _moe_ffn(x1, idx.astype(jnp.int32), gate, w1, w3, w2)
    return _moe_combine(x1, ye, idx.astype(jnp.int32), lg, lb)


def _rope_tables(seq, reps):
    dim = ATT_HEAD_DIM
    inv = 1.0 / (ROPE_THETA ** (jnp.arange(0, dim, 2, dtype=F32) / dim))
    ang = jnp.arange(seq, dtype=F32)[:, None] * inv[None, :]
    ang = jnp.concatenate([ang, ang], -1)
    return jnp.tile(jnp.cos(ang), (1, reps)), jnp.tile(jnp.sin(ang), (1, reps))


def _even_layer(x3, p, s5tabs_fn, lambda_init, l):
    b, s, d = x3.shape
    n = b * s
    x2 = x3.reshape(n, d)
    qk = p['qk']
    av = p['av']
    cos_t, sin_t = _rope_tables(s, 2 * qk // ATT_HEAD_DIM)
    q, k, v, u = _proj_in(x2, p['w_in'], p['w_rot'], cos_t, sin_t, s, qk, av)
    heads = av // ATT_V_DIM
    tk = min(512, s)
    vt = v.reshape(b, s // tk, tk, heads, ATT_V_DIM).transpose(0, 3, 1, 4, 2)
    ot = _diff_attention(q.reshape(b, s, qk), k.reshape(b, s, qk), vt, p['lam'], p['subln_g'], lambda_init, tk)
    attn = ot.transpose(0, 3, 1, 2).reshape(n, av)
    L = min(S5_CHUNK, s)
    nc = s // L
    G, H = p['s5_g'], p['s5_h']
    ug = u.reshape(b, nc, L, G, H).transpose(3, 0, 1, 2, 4).reshape(G, b * nc, L * H).astype(BF16)
    yg = _s5_scan(ug, s5tabs_fn(L, nc), nc)
    yg = yg.reshape(G, b, nc, L, H).transpose(1, 2, 3, 0, 4).reshape(n, G * H)
    return _mix_even(x2, attn, yg, p['glu_w'], p['glu_b'], p['w_out_even'], p['ln_mix_g'][l], p['ln_mix_b'][l],
                     p['w_router'][l])


def _odd_layer(x3, p, l):
    b, s, d = x3.shape
    wc, trig = _dft_tables(s, d // FNET_GROUPS)
    z = _chan_dft(x3, wc).reshape(b, 2 * s, d)
    f = _seq_dft(trig, z, 1.0 / math.sqrt(s * (d // FNET_GROUPS)))
    return _mix_odd(x3.reshape(b * s, d), f.reshape(b * s, d), p['w_out_odd'], p['ln_mix_g'][l], p['ln_mix_b'][l],
                    p['w_router'][l])


def _trunk(x3, p, s5tabs_fn):
    b, s, d = x3.shape
    for l in range(DEPTH):
        if l % 2 == 0:
            lambda_init = 0.8 - 0.6 * math.exp(-0.3 * l)
            x1, aff = _even_layer(x3, p, s5tabs_fn, lambda_init, l)
        else:
            x1, aff = _odd_layer(x3, p, l)
        x2 = _expert_choice(x1, aff, p['w_ff1'][l], p['w_ff3'][l], p['w_ff2'][l], p['ln_ffn_g'][l], p['ln_ffn_b'][l])
        x3 = x2.reshape(b, s, d)
    return x3


def kernel(x_prompt, x_sample, w_in, lam_q1, lam_k1, lam_q2, lam_k2, subln_g, s5_a_re, s5_a_im, s5_log_dt, s5_b_re, s5_b_im, s5_c_re, s5_c_im, s5_d, s5_glu_w, s5_glu_b, w_out_even, w_out_odd, ln_mix_g, ln_mix_b, w_router, w_ff1, w_ff3, w_ff2, ln_ffn_g, ln_ffn_b):
    assert w_in.shape[0] == 1 and w_out_odd.shape[0] == 1 and DEPTH == 2
    d = w_in.shape[1]
    av = 4 * ATT_V_DIM
    qk = av
    wi = w_in[0]
    wqk = wi[:, :2 * qk].reshape(d, 2 * qk // ATT_HEAD_DIM, 2, ATT_HEAD_DIM // 2)
    w_rot = jnp.stack([-wqk[:, :, 1], wqk[:, :, 0]], axis=2).reshape(d, 2 * qk)
    lam = (jnp.exp(jnp.sum(lam_q1[0] * lam_k1[0])) - jnp.exp(jnp.sum(lam_q2[0] * lam_k2[0]))
           + (0.8 - 0.6 * math.exp(0.0))).reshape(1).astype(F32)
    p = {
        'qk': qk, 'av': av, 's5_g': s5_b_re.shape[2], 's5_h': s5_b_re.shape[4],
        'w_in': wi.astype(BF16), 'w_rot': w_rot.astype(BF16), 'lam': lam,
        'subln_g': subln_g[0].reshape(-1, 1),
        'glu_w': s5_glu_w[0].astype(BF16), 'glu_b': s5_glu_b[0].reshape(1, -1),
        'w_out_even': w_out_even[0].astype(BF16), 'w_out_odd': w_out_odd[0].astype(BF16),
        'ln_mix_g': ln_mix_g[:, None, :], 'ln_mix_b': ln_mix_b[:, None, :],
        'ln_ffn_g': ln_ffn_g[:, None, :], 'ln_ffn_b': ln_ffn_b[:, None, :],
        'w_router': w_router,
        'w_ff1': w_ff1.astype(BF16), 'w_ff3': w_ff3.astype(BF16), 'w_ff2': w_ff2.astype(BF16),
    }

    def s5tabs_fn(chunk, nc):
        nsteps = max(1, (nc - 1).bit_length())
        return _s5_tables(s5_a_re[0], s5_a_im[0], s5_log_dt[0], s5_b_re[0], s5_b_im[0], s5_c_re[0], s5_c_im[0],
                          s5_d[0], chunk, nsteps)

    return (_trunk(x_prompt, p, s5tabs_fn), _trunk(x_sample, p, s5tabs_fn))
```

```python
import functools
import math

import jax
import jax.numpy as jnp
from jax import lax
from jax.experimental import pallas as pl
from jax.experimental.pallas import tpu as pltpu

F32 = jnp.float32
BF16 = jnp.bfloat16
HI = lax.Precision.HIGHEST

ATT_HEAD_DIM = 64
ATT_V_DIM = 2 * ATT_HEAD_DIM
FNET_GROUPS = 4
EC_CAPACITY_FACTOR = 2
ROPE_THETA = 10000.0
LN_EPS = 1e-5
DEPTH = 2
DEEPNORM_ALPHA = (2 * DEPTH) ** 0.25

V7X_VMEM_BYTES = 64 * 1024 * 1024
V7X_LANES = 128

S5_CHUNK = 64


def _cparams(semantics, vmem_mib):
    return pltpu.CompilerParams(dimension_semantics=semantics,
                                vmem_limit_bytes=min(vmem_mib * 1024 * 1024, V7X_VMEM_BYTES * 7 // 8))


def _layer_norm(z, g, b):
    mu = jnp.mean(z, -1, keepdims=True)
    zc = z - mu
    var = jnp.mean(zc * zc, -1, keepdims=True)
    return zc * lax.rsqrt(var + LN_EPS) * g + b


def _proj_in_kernel(x_ref, w_ref, wrot_ref, cos_ref, sin_ref, q_ref, k_ref, v_ref, u_ref, *, qk, av, scale):
    xb = x_ref[...].astype(BF16)
    h = jnp.dot(xb, w_ref[...], preferred_element_type=F32)
    hr = jnp.dot(xb, wrot_ref[...], preferred_element_type=F32)
    roped = h[:, :2 * qk] * cos_ref[...] + hr * sin_ref[...]
    q_ref[...] = (roped[:, :qk] * scale).astype(BF16)
    k_ref[...] = roped[:, qk:].astype(BF16)
    v_ref[...] = h[:, 2 * qk:2 * qk + av].astype(BF16)
    u_ref[...] = h[:, 2 * qk + av:]


def _proj_in(x2, w, wrot, cos_t, sin_t, seq, qk, av):
    n, d = x2.shape
    sw = w.shape[1] - 2 * qk - av
    tm = min(512, seq)
    nsb = seq // tm
    kern = functools.partial(_proj_in_kernel, qk=qk, av=av, scale=ATT_HEAD_DIM ** -0.5 * math.log2(math.e))
    return pl.pallas_call(
        kern,
        grid=(n // tm,),
        in_specs=[
            pl.BlockSpec((tm, d), lambda i: (i, 0)),
            pl.BlockSpec(w.shape, lambda i: (0, 0)),
            pl.BlockSpec(wrot.shape, lambda i: (0, 0)),
            pl.BlockSpec((tm, 2 * qk), lambda i: (i % nsb, 0)),
            pl.BlockSpec((tm, 2 * qk), lambda i: (i % nsb, 0)),
        ],
        out_specs=[
            pl.BlockSpec((tm, qk), lambda i: (i, 0)),
            pl.BlockSpec((tm, qk), lambda i: (i, 0)),
            pl.BlockSpec((tm, av), lambda i: (i, 0)),
            pl.BlockSpec((tm, sw), lambda i: (i, 0)),
        ],
        out_shape=[
            jax.ShapeDtypeStruct((n, qk), BF16),
            jax.ShapeDtypeStruct((n, qk), BF16),
            jax.ShapeDtypeStruct((n, av), BF16),
            jax.ShapeDtypeStruct((n, sw), F32),
        ],
        compiler_params=_cparams(("parallel",), 48),
        name="proj_in",
    )(x2, w, wrot, cos_t, sin_t)


def _attn_kernel(lam_ref, q_ref, k_ref, vt_ref, g_ref, o_ref, acc1, acc2, sa1, sa2, sb1, sb2, *, tk, nkv,
                 out_scale):
    q = q_ref[...]
    tq = q.shape[0]
    lane = lax.broadcasted_iota(jnp.int32, q.shape, 1)
    zero = jnp.zeros_like(q)
    q1 = jnp.where(lane < ATT_HEAD_DIM, q, zero)
    q2 = jnp.where(lane >= ATT_HEAD_DIM, q, zero)
    acc1[...] = jnp.zeros_like(acc1)
    acc2[...] = jnp.zeros_like(acc2)
    nt = (((1,), (1,)), ((), ()))

    def scores(j, d1, d2):
        kb = k_ref[pl.ds(pl.multiple_of(j * tk, tk), tk), :]
        d1[...] = lax.dot_general(kb, q1, nt, preferred_element_type=F32)
        d2[...] = lax.dot_general(kb, q2, nt, preferred_element_type=F32)

    def soft_pv(vb, s_ref, m, l, acc):
        s = s_ref[...]
        mn = jnp.maximum(m, jnp.max(s, axis=0, keepdims=True))
        p = jnp.exp2(s - mn)
        a = jnp.exp2(m - mn)
        l = a * l + jnp.sum(p, axis=0, keepdims=True)
        acc[...] = a * acc[...] + jnp.dot(vb, p.astype(BF16), preferred_element_type=F32)
        return mn, l

    def step(j, cur, nxt, carry, prefetch):
        m1, l1, m2, l2 = carry
        if prefetch:
            scores(j + 1, *nxt)
        vb = vt_ref[j]
        m1, l1 = soft_pv(vb, cur[0], m1, l1, acc1)
        m2, l2 = soft_pv(vb, cur[1], m2, l2, acc2)
        return m1, l1, m2, l2

    buf_a, buf_b = (sa1, sa2), (sb1, sb2)

    def body(i, carry):
        carry = step(2 * i, buf_a, buf_b, carry, True)
        return step(2 * i + 1, buf_b, buf_a, carry, True)

    neg = jnp.full((1, tq), -1e30, F32)
    zer = jnp.zeros((1, tq), F32)
    scores(0, *buf_a)
    carry = lax.fori_loop(0, nkv // 2 - 1, body, (neg, zer, neg, zer))
    carry = step(nkv - 2, buf_a, buf_b, carry, True)
    _, l1, _, l2 = step(nkv - 1, buf_b, buf_a, carry, False)
    o = acc1[...] / l1 - lam_ref[0] * (acc2[...] / l2)
    ms = jnp.mean(o * o, axis=0, keepdims=True)
    o = o * lax.rsqrt(ms + LN_EPS) * g_ref[...] * out_scale
    o_ref[...] = o.astype(o_ref.dtype)


def _diff_attention(q, k, vt, lam, g_col, lambda_init, tk):
    b, s, _ = q.shape
    heads = vt.shape[1]
    nkv = vt.shape[2]
    assert nkv % 2 == 0
    dv = vt.shape[3]
    tq = min(512, s)
    kern = functools.partial(_attn_kernel, tk=tk, nkv=nkv, out_scale=1.0 - lambda_init)
    return pl.pallas_call(
        kern,
        grid=(b, heads, s // tq),
        in_specs=[
            pl.BlockSpec(memory_space=pltpu.SMEM),
            pl.BlockSpec((None, tq, 2 * ATT_HEAD_DIM), lambda bi, hi, qi: (bi, qi, hi)),
            pl.BlockSpec((None, s, 2 * ATT_HEAD_DIM), lambda bi, hi, qi: (bi, 0, hi)),
            pl.BlockSpec((None, None, nkv, dv, tk), lambda bi, hi, qi: (bi, hi, 0, 0, 0)),
            pl.BlockSpec((dv, 1), lambda bi, hi, qi: (0, 0)),
        ],
        out_specs=pl.BlockSpec((None, None, dv, tq), lambda bi, hi, qi: (bi, hi, 0, qi)),
        out_shape=jax.ShapeDtypeStruct((b, heads, dv, s), BF16),
        scratch_shapes=[pltpu.VMEM((dv, tq), F32)] * 2 + [pltpu.VMEM((tk, tq), F32)] * 4,
        compiler_params=_cparams(("parallel", "parallel", "arbitrary"), 48),
        name="diff_attn",
    )(lam, q, k, vt, g_col)


def _s5_kernel(u_ref, t_ref, we_ref, ws_ref, a1_ref, a2_ref, y_ref, *, nc, nsteps, p2):
    u = u_ref[...]
    r = u.shape[0]
    e = jnp.dot(u, we_ref[...], preferred_element_type=F32)
    c = lax.broadcasted_iota(jnp.int32, (r, p2), 0) % nc
    half = p2 // 2
    hf = e[:, :p2]
    hb = e[:, p2:]
    for kk in range(nsteps):
        sh = 1 << kk
        pf = jnp.where(c >= sh, pltpu.roll(hf, sh, axis=0), 0.0)
        hf = hf + a1_ref[kk:kk + 1, :p2] * pf + a2_ref[kk:kk + 1, :p2] * pltpu.roll(pf, half, axis=1)
        pb = jnp.where(c < nc - sh, pltpu.roll(hb, r - sh, axis=0), 0.0)
        hb = hb + a1_ref[kk:kk + 1, p2:] * pb + a2_ref[kk:kk + 1, p2:] * pltpu.roll(pb, half, axis=1)
    hf_in = jnp.where(c >= 1, pltpu.roll(hf, 1, axis=0), 0.0)
    hb_in = jnp.where(c < nc - 1, pltpu.roll(hb, r - 1, axis=0), 0.0)
    hin = jnp.concatenate([hf_in, hb_in], axis=1).astype(BF16)
    y = jnp.dot(u, t_ref[...], preferred_element_type=F32)
    y = y + jnp.dot(hin, ws_ref[...], preferred_element_type=F32)
    y_ref[...] = jax.nn.gelu(y)


def _s5_scan(ug, tabs, nc):
    t_tot, we, ws, a1, a2 = tabs
    g, r, lh = ug.shape
    p4 = we.shape[2]
    nsteps = a1.shape[1]
    kern = functools.partial(_s5_kernel, nc=nc, nsteps=nsteps, p2=p4 // 2)
    return pl.pallas_call(
        kern,
        grid=(g,),
        in_specs=[
            pl.BlockSpec((None, r, lh), lambda i: (i, 0, 0)),
            pl.BlockSpec((None, lh, lh), lambda i: (i, 0, 0)),
            pl.BlockSpec((None, lh, p4), lambda i: (i, 0, 0)),
            pl.BlockSpec((None, p4, lh), lambda i: (i, 0, 0)),
            pl.BlockSpec((None, nsteps, p4), lambda i: (i, 0, 0)),
            pl.BlockSpec((None, nsteps, p4), lambda i: (i, 0, 0)),
        ],
        out_specs=pl.BlockSpec((None, r, lh), lambda i: (i, 0, 0)),
        out_shape=jax.ShapeDtypeStruct((g, r, lh), F32),
        compiler_params=_cparams(("parallel",), 48),
        name="s5_scan",
    )(ug, t_tot, we, ws, a1, a2)


def _s5_tables(a_re, a_im, log_dt, b_re, b_im, c_re, c_im, d, chunk, nsteps):
    L = chunk
    G, P = a_re.shape[1], a_re.shape[2]
    H = b_re.shape[3]
    dt = jnp.exp(log_dt)[..., None]
    mag = jnp.exp(a_re * dt)
    lr = mag * jnp.cos(a_im * dt)
    li = mag * jnp.sin(a_im * dt)
    nr = lr - 1.0
    den = a_re * a_re + a_im * a_im
    cr = (nr * a_re + li * a_im) / den
    ci = (li * a_re - nr * a_im) / den
    bbr = cr[..., None] * b_re - ci[..., None] * b_im
    bbi = cr[..., None] * b_im + ci[..., None] * b_re
    j = jnp.arange(L + 1, dtype=F32)[:, None, None, None]
    pm = jnp.exp(j * (a_re * dt)[None])
    ang = j * (a_im * dt)[None]
    pr = pm * jnp.cos(ang)
    pi = pm * jnp.sin(ang)
    cbr = c_re[..., None] * bbr[:, :, None] - c_im[..., None] * bbi[:, :, None]
    cbi = c_re[..., None] * bbi[:, :, None] + c_im[..., None] * bbr[:, :, None]
    kern = (jnp.einsum('jdgp,dghpk->dgjhk', pr[:L], cbr, precision=HI)
            - jnp.einsum('jdgp,dghpk->dgjhk', pi[:L], cbi, precision=HI))
    s_i = jnp.arange(L)[:, None]
    t_i = jnp.arange(L)[None, :]
    lag_f = jnp.clip(t_i - s_i, 0, L - 1)
    lag_b = jnp.clip(s_i - t_i, 0, L - 1)
    tf = kern[0][:, lag_f] * (t_i >= s_i)[None, :, :, None, None]
    tb = kern[1][:, lag_b] * (s_i >= t_i)[None, :, :, None, None]
    skip = (jnp.eye(L, dtype=F32)[None, :, :, None, None] * jnp.eye(H, dtype=F32)[None, None, None]
            * d[:, None, None, :, None])
    t_tot = (tf + tb + skip).transpose(0, 1, 4, 2, 3).reshape(G, L * H, L * H)
    def state_in(pw_r, pw_i, br, bi):
        re = pw_r[..., None] * br[None] - pw_i[..., None] * bi[None]
        im = pw_r[..., None] * bi[None] + pw_i[..., None] * br[None]
        both = jnp.concatenate([re, im], axis=2)
        return both.transpose(1, 0, 3, 2).reshape(G, L * H, 2 * P)
    we = jnp.concatenate([state_in(pr[:L][::-1, 0], pi[:L][::-1, 0], bbr[0], bbi[0]),
                          state_in(pr[:L, 1], pi[:L, 1], bbr[1], bbi[1])], axis=2)
    def state_out(pw_r, pw_i, cre, cim):
        re = cre[None] * pw_r[:, :, None, :] - cim[None] * pw_i[:, :, None, :]
        im = cre[None] * pw_i[:, :, None, :] + cim[None] * pw_r[:, :, None, :]
        both = jnp.concatenate([re, -im], axis=3)
        return both.transpose(1, 3, 0, 2).reshape(G, 2 * P, L * H)
    ws = jnp.concatenate([state_out(pr[1:, 0], pi[1:, 0], c_re[0], c_im[0]),
                          state_out(pr[1:][::-1, 1], pi[1:][::-1, 1], c_re[1], c_im[1])], axis=1)
    ar, ai = pr[L], pi[L]
    a1, a2 = [], []
    for _ in range(nsteps):
        a1.append(jnp.concatenate([ar[0], ar[0], ar[1], ar[1]], axis=-1))
        a2.append(jnp.concatenate([-ai[0], ai[0], -ai[1], ai[1]], axis=-1))
        ar, ai = ar * ar - ai * ai, 2.0 * ar * ai
    a1 = jnp.stack(a1, axis=1)
    a2 = jnp.stack(a2, axis=1)
    return t_tot.astype(BF16), we.astype(BF16), ws.astype(BF16), a1, a2


def _router(x1, wr_ref):
    logits = jnp.dot(x1, wr_ref[...], preferred_element_type=F32, precision=HI)
    logits = logits - jnp.max(logits, axis=-1, keepdims=True)
    ex = jnp.exp(logits)
    return ex / jnp.sum(ex, axis=-1, keepdims=True)


def _mix_even_kernel(x_ref, attn_ref, y_ref, gw_ref, gb_ref, wo_ref, lg_ref, lb_ref, wr_ref,
                     x1_ref, aff_ref, *, aw):
    y = y_ref[...]
    gate = jax.nn.sigmoid(jnp.dot(y.astype(BF16), gw_ref[...], preferred_element_type=F32) + gb_ref[...])
    ssm = (y * gate).astype(BF16)
    m = jnp.dot(attn_ref[...], wo_ref[:aw, :], preferred_element_type=F32)
    m = m + jnp.dot(ssm, wo_ref[aw:, :], preferred_element_type=F32)
    x1 = _layer_norm(DEEPNORM_ALPHA * x_ref[...] + m, lg_ref[...], lb_ref[...])
    x1_ref[...] = x1
    aff_ref[...] = _router(x1, wr_ref)


def _mix_odd_kernel(x_ref, f_ref, wo_ref, lg_ref, lb_ref, wr_ref, x1_ref, aff_ref):
    m = jnp.dot(f_ref[...].astype(BF16), wo_ref[...], preferred_element_type=F32)
    x1 = _layer_norm(DEEPNORM_ALPHA * x_ref[...] + m, lg_ref[...], lb_ref[...])
    x1_ref[...] = x1
    aff_ref[...] = _router(x1, wr_ref)


def _full(a):
    nd = a.ndim
    return pl.BlockSpec(a.shape, lambda i: (0,) * nd)


def _rows(tm, width):
    return pl.BlockSpec((tm, width), lambda i: (i, 0))


def _mix_even(x2, attn, yg, gw, gb, wo, lg, lb, wr):
    n, d = x2.shape
    ne = wr.shape[1]
    tm = min(512, n)
    aw = attn.shape[1]
    return pl.pallas_call(
        functools.partial(_mix_even_kernel, aw=aw),
        grid=(n // tm,),
        in_specs=[_rows(tm, d), _rows(tm, aw), _rows(tm, yg.shape[1]), _full(gw), _full(gb), _full(wo),
                  _full(lg), _full(lb), _full(wr)],
        out_specs=[_rows(tm, d), _rows(tm, ne)],
        out_shape=[jax.ShapeDtypeStruct((n, d), F32), jax.ShapeDtypeStruct((n, ne), F32)],
        compiler_params=_cparams(("parallel",), 48),
        name="mix_even",
    )(x2, attn, yg, gw, gb, wo, lg, lb, wr)


def _mix_odd(x2, f, wo, lg, lb, wr):
    n, d = x2.shape
    ne = wr.shape[1]
    tm = min(512, n)
    return pl.pallas_call(
        _mix_odd_kernel,
        grid=(n // tm,),
        in_specs=[_rows(tm, d), _rows(tm, d), _full(wo), _full(lg), _full(lb), _full(wr)],
        out_specs=[_rows(tm, d), _rows(tm, ne)],
        out_shape=[jax.ShapeDtypeStruct((n, d), F32), jax.ShapeDtypeStruct((n, ne), F32)],
        compiler_params=_cparams(("parallel",), 48),
        name="mix_odd",
    )(x2, f, wo, lg, lb, wr)


def _chan_dft_kernel(x_ref, w_ref, z_ref, *, groups, gw):
    x = x_ref[...].astype(BF16)
    w = w_ref[...]
    for gi in range(groups):
        z = jnp.dot(x[:, gi * gw:(gi + 1) * gw], w, preferred_element_type=F32)
        z_ref[0, :, gi * gw:(gi + 1) * gw] = z[:, :gw].astype(BF16)
        z_ref[1, :, gi * gw:(gi + 1) * gw] = z[:, gw:].astype(BF16)


def _chan_dft(x3, w):
    b, s, d = x3.shape
    tm = min(512, s)
    gw = d // FNET_GROUPS
    return pl.pallas_call(
        functools.partial(_chan_dft_kernel, groups=FNET_GROUPS, gw=gw),
        grid=(b, s // tm),
        in_specs=[pl.BlockSpec((None, tm, d), lambda bi, i: (bi, i, 0)),
                  pl.BlockSpec(w.shape, lambda bi, i: (0, 0))],
        out_specs=pl.BlockSpec((None, 2, tm, d), lambda bi, i: (bi, 0, i, 0)),
        out_shape=jax.ShapeDtypeStruct((b, 2, s, d), BF16),
        compiler_params=_cparams(("parallel", "parallel"), 32),
        name="chan_dft",
    )(x3, w)


def _dft_stage1_kernel(m_ref, z_ref, g_ref):
    g_ref[...] = jnp.dot(m_ref[...], z_ref[...], preferred_element_type=F32)


def _dft_stage1(m1, z):
    b, r, cols = z.shape
    tn = min(8192, cols)
    return pl.pallas_call(
        _dft_stage1_kernel,
        grid=(b, cols // tn),
        in_specs=[pl.BlockSpec(m1.shape, lambda bi, j: (0, 0)),
                  pl.BlockSpec((None, r, tn), lambda bi, j: (bi, 0, j))],
        out_specs=pl.BlockSpec((None, r, tn), lambda bi, j: (bi, 0, j)),
        out_shape=jax.ShapeDtypeStruct((b, r, cols), F32),
        compiler_params=_cparams(("parallel", "parallel"), 32),
        name="dft_stage1",
    )(m1, z)


def _dft_stage2_kernel(m_ref, g_ref, tc_ref, ts_ref, o_ref, *, kb, d, scale):
    m = m_ref[...]
    for kk in range(kb):
        gr = g_ref[0, kk]
        gi = g_ref[1, kk]
        c = tc_ref[kk]
        s = ts_ref[kk]
        rhs = jnp.concatenate([gr * c + gi * s, gi * c - gr * s], axis=0).astype(BF16)
        o_ref[:, kk * d:(kk + 1) * d] = jnp.dot(m, rhs, preferred_element_type=F32) * scale


def _dft_stage2(m2, g5, tw_c, tw_s, scale):
    b, _, s1, s2, d = g5.shape
    kb = min(8, s1)
    return pl.pallas_call(
        functools.partial(_dft_stage2_kernel, kb=kb, d=d, scale=scale),
        grid=(b, s1 // kb),
        in_specs=[pl.BlockSpec(m2.shape, lambda bi, i: (0, 0)),
                  pl.BlockSpec((None, 2, kb, s2, d), lambda bi, i: (bi, 0, i, 0, 0)),
                  pl.BlockSpec((kb, s2, 1), lambda bi, i: (i, 0, 0)),
                  pl.BlockSpec((kb, s2, 1), lambda bi, i: (i, 0, 0))],
        out_specs=pl.BlockSpec((None, s2, kb * d), lambda bi, i: (bi, 0, i)),
        out_shape=jax.ShapeDtypeStruct((b, s2, s1 * d), F32),
        compiler_params=_cparams(("parallel", "parallel"), 48),
        name="dft_stage2",
    )(m2, g5, tw_c, tw_s)


def _trig(rows, cols, period):
    ang = (2.0 * math.pi / period) * ((rows[:, None] * cols[None, :]) % period).astype(F32)
    return jnp.cos(ang), jnp.sin(ang)


def _dft_tables(s1, s2, gw):
    c = jnp.arange(gw, dtype=jnp.int32)
    cc, sc = _trig(c, c, gw)
    wc = jnp.concatenate([cc, -sc], axis=1).astype(BF16)
    i1 = jnp.arange(s1, dtype=jnp.int32)
    i2 = jnp.arange(s2, dtype=jnp.int32)
    c1, sn1 = _trig(i1, i1, s1)
    m1 = jnp.concatenate([jnp.concatenate([c1, sn1], axis=1),
                          jnp.concatenate([-sn1, c1], axis=1)], axis=0).astype(BF16)
    c2, sn2 = _trig(i2, i2, s2)
    m2 = jnp.concatenate([c2, sn2], axis=1).astype(BF16)
    tw_c, tw_s = _trig(i1, i2, s1 * s2)
    return wc, m1, m2, tw_c[..., None], tw_s[..., None]


def _ffn_kernel(idx_cur, idx_nxt, gate_ref, x_hbm, w1_ref, w3_ref, w2_ref, ye_ref, xbuf, sem, *, tm, ffc):
    s = pl.program_id(0)
    ns = pl.num_programs(0)
    slot = s % 2

    def row_copy(idx_ref, r, sl):
        return pltpu.make_async_copy(x_hbm.at[pl.ds(idx_ref[0, 0, r], 1), :], xbuf.at[sl, pl.ds(r, 1), :],
                                     sem.at[sl])

    def wait_block(sl):
        pltpu.make_async_copy(x_hbm.at[pl.ds(0, tm), :], xbuf.at[sl], sem.at[sl]).wait()

    @pl.when(s == 0)
    def _():
        def body(r, carry):
            row_copy(idx_cur, r, 0).start()
            return carry
        lax.fori_loop(0, tm, body, 0, unroll=8)

    wait_block(slot)
    xb = xbuf[slot].astype(BF16)
    for r in range(tm):
        row_copy(idx_nxt, r, 1 - slot).start()
    dff = w1_ref.shape[1]
    acc = jnp.zeros(ye_ref.shape, F32)
    for c0 in range(0, dff, ffc):
        h1 = jnp.dot(xb, w1_ref[:, c0:c0 + ffc], preferred_element_type=F32)
        h3 = jnp.dot(xb, w3_ref[:, c0:c0 + ffc], preferred_element_type=F32)
        gg = (jax.nn.silu(h1) * h3).astype(BF16)
        acc = acc + jnp.dot(gg, w2_ref[c0:c0 + ffc, :], preferred_element_type=F32)
    ye_ref[...] = acc * gate_ref[...]

    @pl.when(s == ns - 1)
    def _():
        wait_block(1 - slot)


def _moe_ffn(x1, idx, gate, w1, w3, w2):
    n, d = x1.shape
    e, cap = idx.shape
    dff = w1.shape[2]
    tm = min(256, cap)
    nblk = cap // tm
    ns = e * nblk
    ffc = dff // 2 if (dff // 2) % V7X_LANES == 0 else dff
    idx3 = idx.reshape(ns, 1, tm)
    gate2 = gate.reshape(e * cap, 1)
    smem_blk = lambda f: pl.BlockSpec((1, 1, tm), f, memory_space=pltpu.SMEM)
    return pl.pallas_call(
        functools.partial(_ffn_kernel, tm=tm, ffc=ffc),
        grid=(ns,),
        in_specs=[
            smem_blk(lambda s: (s, 0, 0)),
            smem_blk(lambda s: (jnp.minimum(s + 1, ns - 1), 0, 0)),
            pl.BlockSpec((tm, 1), lambda s: (s, 0)),
            pl.BlockSpec(memory_space=pl.ANY),
            pl.BlockSpec((None, d, dff), lambda s: (s // nblk, 0, 0)),
            pl.BlockSpec((None, d, dff), lambda s: (s // nblk, 0, 0)),
            pl.BlockSpec((None, dff, d), lambda s: (s // nblk, 0, 0)),
        ],
        out_specs=pl.BlockSpec((tm, d), lambda s: (s, 0)),
        out_shape=jax.ShapeDtypeStruct((e * cap, d), F32),
        scratch_shapes=[pltpu.VMEM((2, tm, d), F32), pltpu.SemaphoreType.DMA((2,))],
        compiler_params=_cparams(("arbitrary",), 56),
        name="moe_ffn",
    )(idx3, idx3, gate2, x1, w1, w3, w2)


def _combine_kernel(tile_ref, chunk_ref, first_ref, last_ref, valid_ref,
                    slot_cur, slot_nxt, tok_ref, x_ref, ye_hbm, lg_ref, lb_ref, o_ref, acc_ref, buf, sem, *, tt, ca):
    w = pl.program_id(0)
    nw = pl.num_programs(0)
    par = w % 2

    def gather(slot_ref, sl):
        def body(r, carry):
            a = slot_ref[0, 0, r]
            pltpu.make_async_copy(ye_hbm.at[pl.ds(a, 1), :], buf.at[sl, pl.ds(r, 1), :], sem.at[sl]).start()
            return carry
        lax.fori_loop(0, ca, body, 0, unroll=8)

    @pl.when((w == 0) & (valid_ref[0] == 1))
    def _():
        gather(slot_cur, 0)

    @pl.when(valid_ref[jnp.minimum(w + 1, nw - 1)] * (w + 1 < nw).astype(jnp.int32) == 1)
    def _():
        gather(slot_nxt, 1 - par)

    @pl.when(first_ref[w] == 1)
    def _():
        acc_ref[...] = jnp.zeros_like(acc_ref)

    @pl.when(valid_ref[w] == 1)
    def _():
        pltpu.make_async_copy(ye_hbm.at[pl.ds(0, ca), :], buf.at[par], sem.at[par]).wait()
        t0 = tile_ref[w] * tt
        row = lax.broadcasted_iota(jnp.int32, (tt, ca), 0) + t0
        onehot = jnp.where(tok_ref[0] == row, 1.0, 0.0).astype(BF16)
        acc_ref[...] += jnp.dot(onehot, buf[par].astype(BF16), preferred_element_type=F32)

    @pl.when(last_ref[w] == 1)
    def _():
        o_ref[...] = _layer_norm(DEEPNORM_ALPHA * x_ref[...] + acc_ref[...], lg_ref[...], lb_ref[...])


def _moe_combine(x1, ye, idx, lg, lb):
    n, d = x1.shape
    na = ye.shape[0]
    tt = min(256, n)
    ca = min(256, na)
    nt = n // tt
    nch = na // ca
    nw = nt + nch
    tok_flat = idx.reshape(-1)
    order = jnp.argsort(tok_flat).astype(jnp.int32)
    tok_sorted = tok_flat[order]
    bounds = jnp.searchsorted(tok_sorted, jnp.arange(nt + 1, dtype=jnp.int32) * tt, side='left').astype(jnp.int32)
    start, end = bounds[:-1], bounds[1:]
    c_lo = jnp.minimum(start // ca, nch - 1)
    c_hi = jnp.where(end > start, (end - 1) // ca, c_lo)
    cnt = c_hi - c_lo + 1
    off = jnp.cumsum(cnt) - cnt
    total = off[-1] + cnt[-1]
    wi = jnp.arange(nw, dtype=jnp.int32)
    tile_w = jnp.clip(jnp.searchsorted(off, wi, side='right').astype(jnp.int32) - 1, 0, nt - 1)
    valid = wi < total
    rel = wi - off[tile_w]
    chunk_w = jnp.where(valid, c_lo[tile_w] + rel, nch - 1).astype(jnp.int32)
    first = (valid & (rel == 0)).astype(jnp.int32)
    last = (valid & (rel == cnt[tile_w] - 1)).astype(jnp.int32)
    tile_w = jnp.where(valid, tile_w, nt - 1).astype(jnp.int32)
    valid = valid.astype(jnp.int32)
    slot3 = order.reshape(nch, 1, ca)
    tok3 = tok_sorted.reshape(nch, 1, ca)
    grid_spec = pltpu.PrefetchScalarGridSpec(
        num_scalar_prefetch=5,
        grid=(nw,),
        in_specs=[
            pl.BlockSpec((1, 1, ca), lambda w, tl, ch, fi, la, va: (ch[w], 0, 0), memory_space=pltpu.SMEM),
            pl.BlockSpec((1, 1, ca), lambda w, tl, ch, fi, la, va: (ch[jnp.minimum(w + 1, nw - 1)], 0, 0),
                         memory_space=pltpu.SMEM),
            pl.BlockSpec((1, 1, ca), lambda w, tl, ch, fi, la, va: (ch[w], 0, 0)),
            pl.BlockSpec((tt, d), lambda w, tl, ch, fi, la, va: (tl[w], 0)),
            pl.BlockSpec(memory_space=pl.ANY),
            pl.BlockSpec((1, d), lambda w, tl, ch, fi, la, va: (0, 0)),
            pl.BlockSpec((1, d), lambda w, tl, ch, fi, la, va: (0, 0)),
        ],
        out_specs=pl.BlockSpec((tt, d), lambda w, tl, ch, fi, la, va: (tl[w], 0)),
        scratch_shapes=[pltpu.VMEM((tt, d), F32), pltpu.VMEM((2, ca, d), F32), pltpu.SemaphoreType.DMA((2,))],
    )
    return pl.pallas_call(
        functools.partial(_combine_kernel, tt=tt, ca=ca),
        grid_spec=grid_spec,
        out_shape=jax.ShapeDtypeStruct((n, d), F32),
        compiler_params=_cparams(("arbitrary",), 32),
        name="moe_combine",
    )(tile_w, chunk_w, first, last, valid, slot3, slot3, tok3, x1, ye, lg, lb)


def _expert_choice(x1, aff, w1, w3, w2, lg, lb):
    n = x1.shape[0]
    ne = aff.shape[1]
    cap = max(1, EC_CAPACITY_FACTOR * n // ne)
    gate, idx = lax.top_k(aff.T, cap)
    ye = _moe_ffn(x1, idx.astype(jnp.int32), gate, w1, w3, w2)
    return _moe_combine(x1, ye, idx.astype(jnp.int32), lg, lb)


def _rope_tables(seq, reps):
    dim = ATT_HEAD_DIM
    inv = 1.0 / (ROPE_THETA ** (jnp.arange(0, dim, 2, dtype=F32) / dim))
    ang = jnp.arange(seq, dtype=F32)[:, None] * inv[None, :]
    ang = jnp.concatenate([ang, ang], -1)
    return jnp.tile(jnp.cos(ang), (1, reps)), jnp.tile(jnp.sin(ang), (1, reps))


def _even_layer(x3, p, s5tabs_fn, lambda_init, l):
    b, s, d = x3.shape
    n = b * s
    x2 = x3.reshape(n, d)
    qk = p['qk']
    av = p['av']
    cos_t, sin_t = _rope_tables(s, 2 * qk // ATT_HEAD_DIM)
    q, k, v, u = _proj_in(x2, p['w_in'], p['w_rot'], cos_t, sin_t, s, qk, av)
    heads = av // ATT_V_DIM
    tk = min(512, s // 2)
    vt = v.reshape(b, s // tk, tk, heads, ATT_V_DIM).transpose(0, 3, 1, 4, 2)
    ot = _diff_attention(q.reshape(b, s, qk), k.reshape(b, s, qk), vt, p['lam'], p['subln_g'], lambda_init, tk)
    attn = ot.transpose(0, 3, 1, 2).reshape(n, av)
    L = min(S5_CHUNK, s)
    nc = s // L
    G, H = p['s5_g'], p['s5_h']
    ug = u.reshape(b, nc, L, G, H).transpose(3, 0, 1, 2, 4).reshape(G, b * nc, L * H).astype(BF16)
    yg = _s5_scan(ug, s5tabs_fn(L, nc), nc)
    yg = yg.reshape(G, b, nc, L, H).transpose(1, 2, 3, 0, 4).reshape(n, G * H)
    return _mix_even(x2, attn, yg, p['glu_w'], p['glu_b'], p['w_out_even'], p['ln_mix_g'][l], p['ln_mix_b'][l],
                     p['w_router'][l])


def _odd_layer(x3, p, l):
    b, s, d = x3.shape
    gw = d // FNET_GROUPS
    s2 = 128 if s >= 1024 else s // 8
    s1 = s // s2
    wc, m1, m2, tw_c, tw_s = _dft_tables(s1, s2, gw)
    z = _chan_dft(x3, wc).reshape(b, 2 * s1, s2 * d)
    g = _dft_stage1(m1, z).reshape(b, 2, s1, s2, d)
    f = _dft_stage2(m2, g, tw_c, tw_s, 1.0 / math.sqrt(s * gw))
    return _mix_odd(x3.reshape(b * s, d), f.reshape(b * s, d), p['w_out_odd'], p['ln_mix_g'][l], p['ln_mix_b'][l],
                    p['w_router'][l])


def _trunk(x3, p, s5tabs_fn):
    b, s, d = x3.shape
    for l in range(DEPTH):
        if l % 2 == 0:
            lambda_init = 0.8 - 0.6 * math.exp(-0.3 * l)
            x1, aff = _even_layer(x3, p, s5tabs_fn, lambda_init, l)
        else:
            x1, aff = _odd_layer(x3, p, l)
        x2 = _expert_choice(x1, aff, p['w_ff1'][l], p['w_ff3'][l], p['w_ff2'][l], p['ln_ffn_g'][l], p['ln_ffn_b'][l])
        x3 = x2.reshape(b, s, d)
    return x3


def kernel(x_prompt, x_sample, w_in, lam_q1, lam_k1, lam_q2, lam_k2, subln_g, s5_a_re, s5_a_im, s5_log_dt, s5_b_re, s5_b_im, s5_c_re, s5_c_im, s5_d, s5_glu_w, s5_glu_b, w_out_even, w_out_odd, ln_mix_g, ln_mix_b, w_router, w_ff1, w_ff3, w_ff2, ln_ffn_g, ln_ffn_b):
    assert w_in.shape[0] == 1 and w_out_odd.shape[0] == 1 and DEPTH == 2
    d = w_in.shape[1]
    av = 4 * ATT_V_DIM
    qk = av
    wi = w_in[0]
    wqk = wi[:, :2 * qk].reshape(d, 2 * qk // ATT_HEAD_DIM, 2, ATT_HEAD_DIM // 2)
    w_rot = jnp.stack([-wqk[:, :, 1], wqk[:, :, 0]], axis=2).reshape(d, 2 * qk)
    lam = (jnp.exp(jnp.sum(lam_q1[0] * lam_k1[0])) - jnp.exp(jnp.sum(lam_q2[0] * lam_k2[0]))
           + (0.8 - 0.6 * math.exp(0.0))).reshape(1).astype(F32)
    p = {
        'qk': qk, 'av': av, 's5_g': s5_b_re.shape[2], 's5_h': s5_b_re.shape[4],
        'w_in': wi.astype(BF16), 'w_rot': w_rot.astype(BF16), 'lam': lam,
        'subln_g': subln_g[0].reshape(-1, 1),
        'glu_w': s5_glu_w[0].astype(BF16), 'glu_b': s5_glu_b[0].reshape(1, -1),
        'w_out_even': w_out_even[0].astype(BF16), 'w_out_odd': w_out_odd[0].astype(BF16),
        'ln_mix_g': ln_mix_g[:, None, :], 'ln_mix_b': ln_mix_b[:, None, :],
        'ln_ffn_g': ln_ffn_g[:, None, :], 'ln_ffn_b': ln_ffn_b[:, None, :],
        'w_router': w_router,
        'w_ff1': w_ff1.astype(BF16), 'w_ff3': w_ff3.astype(BF16), 'w_ff2': w_ff2.astype(BF16),
    }

    def s5tabs_fn(chunk, nc):
        nsteps = max(1, (nc - 1).bit_length())
        return _s5_tables(s5_a_re[0], s5_a_im[0], s5_log_dt[0], s5_b_re[0], s5_b_im[0], s5_c_re[0], s5_c_im[0],
                          s5_d[0], chunk, nsteps)

    return (_trunk(x_prompt, p, s5tabs_fn), _trunk(x_sample, p, s5tabs_fn))
```

```python
import functools
import math

import jax
import jax.numpy as jnp
from jax import lax
from jax.experimental import pallas as pl
from jax.experimental.pallas import tpu as pltpu

F32 = jnp.float32
BF16 = jnp.bfloat16
HI = lax.Precision.HIGHEST

ATT_HEAD_DIM = 64
ATT_V_DIM = 2 * ATT_HEAD_DIM
FNET_GROUPS = 4
EC_CAPACITY_FACTOR = 2
ROPE_THETA = 10000.0
LN_EPS = 1e-5
DEPTH = 2
DEEPNORM_ALPHA = (2 * DEPTH) ** 0.25

V7X_VMEM_BYTES = 64 * 1024 * 1024
V7X_LANES = 128

S5_CHUNK = 64


def _cparams(semantics, vmem_mib):
    return pltpu.CompilerParams(dimension_semantics=semantics,
                                vmem_limit_bytes=min(vmem_mib * 1024 * 1024, V7X_VMEM_BYTES * 7 // 8))


def _layer_norm(z, g, b):
    mu = jnp.mean(z, -1, keepdims=True)
    zc = z - mu
    var = jnp.mean(zc * zc, -1, keepdims=True)
    return zc * lax.rsqrt(var + LN_EPS) * g + b


def _proj_in_kernel(x_ref, w_ref, wrot_ref, cos_ref, sin_ref, q_ref, k_ref, v_ref, u_ref, *, qk, av, scale):
    xb = x_ref[...].astype(BF16)
    h = jnp.dot(xb, w_ref[...], preferred_element_type=F32)
    hr = jnp.dot(xb, wrot_ref[...], preferred_element_type=F32)
    roped = h[:, :2 * qk] * cos_ref[...] + hr * sin_ref[...]
    q_ref[...] = (roped[:, :qk] * scale).astype(BF16)
    k_ref[...] = roped[:, qk:].astype(BF16)
    v_ref[...] = h[:, 2 * qk:2 * qk + av].astype(BF16)
    u_ref[...] = h[:, 2 * qk + av:].astype(BF16)


def _proj_in(x2, w, wrot, cos_t, sin_t, seq, qk, av):
    n, d = x2.shape
    sw = w.shape[1] - 2 * qk - av
    tm = min(512, seq)
    nsb = seq // tm
    kern = functools.partial(_proj_in_kernel, qk=qk, av=av, scale=ATT_HEAD_DIM ** -0.5 * math.log2(math.e))
    return pl.pallas_call(
        kern,
        grid=(n // tm,),
        in_specs=[
            pl.BlockSpec((tm, d), lambda i: (i, 0)),
            pl.BlockSpec(w.shape, lambda i: (0, 0)),
            pl.BlockSpec(wrot.shape, lambda i: (0, 0)),
            pl.BlockSpec((tm, 2 * qk), lambda i: (i % nsb, 0)),
            pl.BlockSpec((tm, 2 * qk), lambda i: (i % nsb, 0)),
        ],
        out_specs=[
            pl.BlockSpec((tm, qk), lambda i: (i, 0)),
            pl.BlockSpec((tm, qk), lambda i: (i, 0)),
            pl.BlockSpec((tm, av), lambda i: (i, 0)),
            pl.BlockSpec((tm, sw), lambda i: (i, 0)),
        ],
        out_shape=[
            jax.ShapeDtypeStruct((n, qk), BF16),
            jax.ShapeDtypeStruct((n, qk), BF16),
            jax.ShapeDtypeStruct((n, av), BF16),
            jax.ShapeDtypeStruct((n, sw), BF16),
        ],
        compiler_params=_cparams(("parallel",), 48),
        name="proj_in",
    )(x2, w, wrot, cos_t, sin_t)


def _attn_kernel(lam_ref, q_ref, k_ref, vt_ref, g_ref, o_ref, acc1, acc2, sa1, sa2, sb1, sb2, *, tk, nkv,
                 out_scale):
    q = q_ref[...]
    tq = q.shape[0]
    lane = lax.broadcasted_iota(jnp.int32, q.shape, 1)
    zero = jnp.zeros_like(q)
    q1 = jnp.where(lane < ATT_HEAD_DIM, q, zero)
    q2 = jnp.where(lane >= ATT_HEAD_DIM, q, zero)
    acc1[...] = jnp.zeros_like(acc1)
    acc2[...] = jnp.zeros_like(acc2)
    nt = (((1,), (1,)), ((), ()))

    def scores(j, d1, d2):
        kb = k_ref[pl.ds(pl.multiple_of(j * tk, tk), tk), :]
        d1[...] = lax.dot_general(kb, q1, nt, preferred_element_type=F32)
        d2[...] = lax.dot_general(kb, q2, nt, preferred_element_type=F32)

    def soft_pv(vb, s_ref, m, l, acc):
        s = s_ref[...]
        mn = jnp.maximum(m, jnp.max(s, axis=0, keepdims=True))
        p = jnp.exp2(s - mn)
        a = jnp.exp2(m - mn)
        l = a * l + jnp.sum(p, axis=0, keepdims=True)
        acc[...] = a * acc[...] + jnp.dot(vb, p.astype(BF16), preferred_element_type=F32)
        return mn, l

    def step(j, cur, nxt, carry, prefetch):
        m1, l1, m2, l2 = carry
        if prefetch:
            scores(j + 1, *nxt)
        vb = vt_ref[j]
        m1, l1 = soft_pv(vb, cur[0], m1, l1, acc1)
        m2, l2 = soft_pv(vb, cur[1], m2, l2, acc2)
        return m1, l1, m2, l2

    buf_a, buf_b = (sa1, sa2), (sb1, sb2)

    def body(i, carry):
        carry = step(2 * i, buf_a, buf_b, carry, True)
        return step(2 * i + 1, buf_b, buf_a, carry, True)

    neg = jnp.full((1, tq), -1e30, F32)
    zer = jnp.zeros((1, tq), F32)
    scores(0, *buf_a)
    carry = lax.fori_loop(0, nkv // 2 - 1, body, (neg, zer, neg, zer))
    carry = step(nkv - 2, buf_a, buf_b, carry, True)
    _, l1, _, l2 = step(nkv - 1, buf_b, buf_a, carry, False)
    o = acc1[...] / l1 - lam_ref[0] * (acc2[...] / l2)
    ms = jnp.mean(o * o, axis=0, keepdims=True)
    o = o * lax.rsqrt(ms + LN_EPS) * g_ref[...] * out_scale
    o_ref[...] = o.astype(o_ref.dtype)


def _diff_attention(q, k, vt, lam, g_col, lambda_init, tk):
    b, s, _ = q.shape
    heads = vt.shape[1]
    nkv = vt.shape[2]
    assert nkv % 2 == 0
    dv = vt.shape[3]
    tq = min(1024, s)
    kern = functools.partial(_attn_kernel, tk=tk, nkv=nkv, out_scale=1.0 - lambda_init)
    return pl.pallas_call(
        kern,
        grid=(b, heads, s // tq),
        in_specs=[
            pl.BlockSpec(memory_space=pltpu.SMEM),
            pl.BlockSpec((None, tq, 2 * ATT_HEAD_DIM), lambda bi, hi, qi: (bi, qi, hi)),
            pl.BlockSpec((None, s, 2 * ATT_HEAD_DIM), lambda bi, hi, qi: (bi, 0, hi)),
            pl.BlockSpec((None, None, nkv, dv, tk), lambda bi, hi, qi: (bi, hi, 0, 0, 0)),
            pl.BlockSpec((dv, 1), lambda bi, hi, qi: (0, 0)),
        ],
        out_specs=pl.BlockSpec((None, None, dv, tq), lambda bi, hi, qi: (bi, hi, 0, qi)),
        out_shape=jax.ShapeDtypeStruct((b, heads, dv, s), BF16),
        scratch_shapes=[pltpu.VMEM((dv, tq), F32)] * 2 + [pltpu.VMEM((tk, tq), F32)] * 4,
        compiler_params=_cparams(("parallel", "parallel", "arbitrary"), 48),
        name="diff_attn",
    )(lam, q, k, vt, g_col)


def _s5_kernel(u_ref, t_ref, we_ref, ws_ref, a1_ref, a2_ref, y_ref, *, nc, nsteps, p2):
    u = u_ref[...]
    r = u.shape[0]
    e = jnp.dot(u, we_ref[...], preferred_element_type=F32)
    c = lax.broadcasted_iota(jnp.int32, (r, p2), 0) % nc
    half = p2 // 2
    hf = e[:, :p2]
    hb = e[:, p2:]
    for kk in range(nsteps):
        sh = 1 << kk
        pf = jnp.where(c >= sh, pltpu.roll(hf, sh, axis=0), 0.0)
        hf = hf + a1_ref[kk:kk + 1, :p2] * pf + a2_ref[kk:kk + 1, :p2] * pltpu.roll(pf, half, axis=1)
        pb = jnp.where(c < nc - sh, pltpu.roll(hb, r - sh, axis=0), 0.0)
        hb = hb + a1_ref[kk:kk + 1, p2:] * pb + a2_ref[kk:kk + 1, p2:] * pltpu.roll(pb, half, axis=1)
    hf_in = jnp.where(c >= 1, pltpu.roll(hf, 1, axis=0), 0.0)
    hb_in = jnp.where(c < nc - 1, pltpu.roll(hb, r - 1, axis=0), 0.0)
    hin = jnp.concatenate([hf_in, hb_in], axis=1).astype(BF16)
    y = jnp.dot(u, t_ref[...], preferred_element_type=F32)
    y = y + jnp.dot(hin, ws_ref[...], preferred_element_type=F32)
    y_ref[...] = jax.nn.gelu(y).astype(BF16)


def _s5_scan(ug, tabs, nc):
    t_tot, we, ws, a1, a2 = tabs
    g, r, lh = ug.shape
    p4 = we.shape[2]
    nsteps = a1.shape[1]
    kern = functools.partial(_s5_kernel, nc=nc, nsteps=nsteps, p2=p4 // 2)
    return pl.pallas_call(
        kern,
        grid=(g,),
        in_specs=[
            pl.BlockSpec((None, r, lh), lambda i: (i, 0, 0)),
            pl.BlockSpec((None, lh, lh), lambda i: (i, 0, 0)),
            pl.BlockSpec((None, lh, p4), lambda i: (i, 0, 0)),
            pl.BlockSpec((None, p4, lh), lambda i: (i, 0, 0)),
            pl.BlockSpec((None, nsteps, p4), lambda i: (i, 0, 0)),
            pl.BlockSpec((None, nsteps, p4), lambda i: (i, 0, 0)),
        ],
        out_specs=pl.BlockSpec((None, r, lh), lambda i: (i, 0, 0)),
        out_shape=jax.ShapeDtypeStruct((g, r, lh), BF16),
        compiler_params=_cparams(("parallel",), 48),
        name="s5_scan",
    )(ug, t_tot, we, ws, a1, a2)


def _s5_tables(a_re, a_im, log_dt, b_re, b_im, c_re, c_im, d, chunk, nsteps):
    L = chunk
    G, P = a_re.shape[1], a_re.shape[2]
    H = b_re.shape[3]
    dt = jnp.exp(log_dt)[..., None]
    mag = jnp.exp(a_re * dt)
    lr = mag * jnp.cos(a_im * dt)
    li = mag * jnp.sin(a_im * dt)
    nr = lr - 1.0
    den = a_re * a_re + a_im * a_im
    cr = (nr * a_re + li * a_im) / den
    ci = (li * a_re - nr * a_im) / den
    bbr = cr[..., None] * b_re - ci[..., None] * b_im
    bbi = cr[..., None] * b_im + ci[..., None] * b_re
    j = jnp.arange(L + 1, dtype=F32)[:, None, None, None]
    pm = jnp.exp(j * (a_re * dt)[None])
    ang = j * (a_im * dt)[None]
    pr = pm * jnp.cos(ang)
    pi = pm * jnp.sin(ang)
    cbr = c_re[..., None] * bbr[:, :, None] - c_im[..., None] * bbi[:, :, None]
    cbi = c_re[..., None] * bbi[:, :, None] + c_im[..., None] * bbr[:, :, None]
    kern = (jnp.einsum('jdgp,dghpk->dgjhk', pr[:L], cbr, precision=HI)
            - jnp.einsum('jdgp,dghpk->dgjhk', pi[:L], cbi, precision=HI))
    s_i = jnp.arange(L)[:, None]
    t_i = jnp.arange(L)[None, :]
    lag_f = jnp.clip(t_i - s_i, 0, L - 1)
    lag_b = jnp.clip(s_i - t_i, 0, L - 1)
    tf = kern[0][:, lag_f] * (t_i >= s_i)[None, :, :, None, None]
    tb = kern[1][:, lag_b] * (s_i >= t_i)[None, :, :, None, None]
    skip = (jnp.eye(L, dtype=F32)[None, :, :, None, None] * jnp.eye(H, dtype=F32)[None, None, None]
            * d[:, None, None, :, None])
    t_tot = (tf + tb + skip).transpose(0, 1, 4, 2, 3).reshape(G, L * H, L * H)
    def state_in(pw_r, pw_i, br, bi):
        re = pw_r[..., None] * br[None] - pw_i[..., None] * bi[None]
        im = pw_r[..., None] * bi[None] + pw_i[..., None] * br[None]
        both = jnp.concatenate([re, im], axis=2)
        return both.transpose(1, 0, 3, 2).reshape(G, L * H, 2 * P)
    we = jnp.concatenate([state_in(pr[:L][::-1, 0], pi[:L][::-1, 0], bbr[0], bbi[0]),
                          state_in(pr[:L, 1], pi[:L, 1], bbr[1], bbi[1])], axis=2)
    def state_out(pw_r, pw_i, cre, cim):
        re = cre[None] * pw_r[:, :, None, :] - cim[None] * pw_i[:, :, None, :]
        im = cre[None] * pw_i[:, :, None, :] + cim[None] * pw_r[:, :, None, :]
        both = jnp.concatenate([re, -im], axis=3)
        return both.transpose(1, 3, 0, 2).reshape(G, 2 * P, L * H)
    ws = jnp.concatenate([state_out(pr[1:, 0], pi[1:, 0], c_re[0], c_im[0]),
                          state_out(pr[1:][::-1, 1], pi[1:][::-1, 1], c_re[1], c_im[1])], axis=1)
    ar, ai = pr[L], pi[L]
    a1, a2 = [], []
    for _ in range(nsteps):
        a1.append(jnp.concatenate([ar[0], ar[0], ar[1], ar[1]], axis=-1))
        a2.append(jnp.concatenate([-ai[0], ai[0], -ai[1], ai[1]], axis=-1))
        ar, ai = ar * ar - ai * ai, 2.0 * ar * ai
    a1 = jnp.stack(a1, axis=1)
    a2 = jnp.stack(a2, axis=1)
    return t_tot.astype(BF16), we.astype(BF16), ws.astype(BF16), a1, a2


def _router_t(x1, wrt_ref):
    logits = lax.dot_general(wrt_ref[...], x1, (((1,), (1,)), ((), ())), preferred_element_type=F32, precision=HI)
    logits = logits - jnp.max(logits, axis=0, keepdims=True)
    ex = jnp.exp(logits)
    return ex / jnp.sum(ex, axis=0, keepdims=True)


def _mix_even_kernel(x_ref, attn_ref, y_ref, gw_ref, gb_ref, wo_ref, lg_ref, lb_ref, wrt_ref,
                     x1_ref, aff_ref, *, aw):
    y = y_ref[...]
    gate = jax.nn.sigmoid(jnp.dot(y, gw_ref[...], preferred_element_type=F32) + gb_ref[...])
    ssm = (y.astype(F32) * gate).astype(BF16)
    m = jnp.dot(attn_ref[...], wo_ref[:aw, :], preferred_element_type=F32)
    m = m + jnp.dot(ssm, wo_ref[aw:, :], preferred_element_type=F32)
    x1 = _layer_norm(DEEPNORM_ALPHA * x_ref[...] + m, lg_ref[...], lb_ref[...])
    x1_ref[...] = x1
    aff_ref[...] = _router_t(x1, wrt_ref)


def _mix_odd_kernel(x_ref, f_ref, wo_ref, lg_ref, lb_ref, wrt_ref, x1_ref, aff_ref):
    m = jnp.dot(f_ref[...].astype(BF16), wo_ref[...], preferred_element_type=F32)
    x1 = _layer_norm(DEEPNORM_ALPHA * x_ref[...] + m, lg_ref[...], lb_ref[...])
    x1_ref[...] = x1
    aff_ref[...] = _router_t(x1, wrt_ref)


def _full(a):
    nd = a.ndim
    return pl.BlockSpec(a.shape, lambda i: (0,) * nd)


def _rows(tm, width):
    return pl.BlockSpec((tm, width), lambda i: (i, 0))


def _mix_out(n, d, ne, tm):
    specs = [pl.BlockSpec((tm, d), lambda i: (i, 0)), pl.BlockSpec((ne, tm), lambda i: (0, i))]
    shapes = [jax.ShapeDtypeStruct((n, d), F32), jax.ShapeDtypeStruct((ne, n), F32)]
    return specs, shapes


def _mix_even(x2, attn, yg, gw, gb, wo, lg, lb, wrt):
    n, d = x2.shape
    tm = min(512, n)
    aw = attn.shape[1]
    out_specs, out_shape = _mix_out(n, d, wrt.shape[0], tm)
    return pl.pallas_call(
        functools.partial(_mix_even_kernel, aw=aw),
        grid=(n // tm,),
        in_specs=[_rows(tm, d), _rows(tm, aw), _rows(tm, yg.shape[1]), _full(gw), _full(gb), _full(wo),
                  _full(lg), _full(lb), _full(wrt)],
        out_specs=out_specs,
        out_shape=out_shape,
        compiler_params=_cparams(("parallel",), 48),
        name="mix_even",
    )(x2, attn, yg, gw, gb, wo, lg, lb, wrt)


def _mix_odd(x2, f, wo, lg, lb, wrt):
    n, d = x2.shape
    tm = min(512, n)
    out_specs, out_shape = _mix_out(n, d, wrt.shape[0], tm)
    return pl.pallas_call(
        _mix_odd_kernel,
        grid=(n // tm,),
        in_specs=[_rows(tm, d), _rows(tm, d), _full(wo), _full(lg), _full(lb), _full(wrt)],
        out_specs=out_specs,
        out_shape=out_shape,
        compiler_params=_cparams(("parallel",), 48),
        name="mix_odd",
    )(x2, f, wo, lg, lb, wrt)


def _chan_dft_kernel(x_ref, w_ref, z_ref, *, groups, gw):
    x = x_ref[...].astype(BF16)
    w = w_ref[...]
    for gi in range(groups):
        z = jnp.dot(x[:, gi * gw:(gi + 1) * gw], w, preferred_element_type=F32)
        z_ref[0, :, gi * gw:(gi + 1) * gw] = z[:, :gw].astype(BF16)
        z_ref[1, :, gi * gw:(gi + 1) * gw] = z[:, gw:].astype(BF16)


def _chan_dft(x3, w):
    b, s, d = x3.shape
    tm = min(512, s)
    gw = d // FNET_GROUPS
    return pl.pallas_call(
        functools.partial(_chan_dft_kernel, groups=FNET_GROUPS, gw=gw),
        grid=(b, s // tm),
        in_specs=[pl.BlockSpec((None, tm, d), lambda bi, i: (bi, i, 0)),
                  pl.BlockSpec(w.shape, lambda bi, i: (0, 0))],
        out_specs=pl.BlockSpec((None, 2, tm, d), lambda bi, i: (bi, 0, i, 0)),
        out_shape=jax.ShapeDtypeStruct((b, 2, s, d), BF16),
        compiler_params=_cparams(("parallel", "parallel"), 32),
        name="chan_dft",
    )(x3, w)


def _dft_stage1_kernel(m_ref, z_ref, g_ref):
    g_ref[...] = jnp.dot(m_ref[...], z_ref[...], preferred_element_type=F32)


def _dft_stage1(m1, z):
    b, r, cols = z.shape
    tn = min(8192, cols)
    return pl.pallas_call(
        _dft_stage1_kernel,
        grid=(b, cols // tn),
        in_specs=[pl.BlockSpec(m1.shape, lambda bi, j: (0, 0)),
                  pl.BlockSpec((None, r, tn), lambda bi, j: (bi, 0, j))],
        out_specs=pl.BlockSpec((None, r, tn), lambda bi, j: (bi, 0, j)),
        out_shape=jax.ShapeDtypeStruct((b, r, cols), F32),
        compiler_params=_cparams(("parallel", "parallel"), 32),
        name="dft_stage1",
    )(m1, z)


def _dft_stage2_kernel(m_ref, g_ref, tc_ref, ts_ref, o_ref, *, kb, d, scale):
    m = m_ref[...]
    for kk in range(kb):
        gr = g_ref[0, kk]
        gi = g_ref[1, kk]
        c = tc_ref[kk]
        s = ts_ref[kk]
        rhs = jnp.concatenate([gr * c + gi * s, gi * c - gr * s], axis=0).astype(BF16)
        o_ref[:, kk * d:(kk + 1) * d] = jnp.dot(m, rhs, preferred_element_type=F32) * scale


def _dft_stage2(m2, g5, tw_c, tw_s, scale):
    b, _, s1, s2, d = g5.shape
    kb = min(8, s1)
    return pl.pallas_call(
        functools.partial(_dft_stage2_kernel, kb=kb, d=d, scale=scale),
        grid=(b, s1 // kb),
        in_specs=[pl.BlockSpec(m2.shape, lambda bi, i: (0, 0)),
                  pl.BlockSpec((None, 2, kb, s2, d), lambda bi, i: (bi, 0, i, 0, 0)),
                  pl.BlockSpec((kb, s2, 1), lambda bi, i: (i, 0, 0)),
                  pl.BlockSpec((kb, s2, 1), lambda bi, i: (i, 0, 0))],
        out_specs=pl.BlockSpec((None, s2, kb * d), lambda bi, i: (bi, 0, i)),
        out_shape=jax.ShapeDtypeStruct((b, s2, s1 * d), F32),
        compiler_params=_cparams(("parallel", "parallel"), 48),
        name="dft_stage2",
    )(m2, g5, tw_c, tw_s)


def _trig(rows, cols, period):
    ang = (2.0 * math.pi / period) * ((rows[:, None] * cols[None, :]) % period).astype(F32)
    return jnp.cos(ang), jnp.sin(ang)


def _dft_tables(s1, s2, gw):
    c = jnp.arange(gw, dtype=jnp.int32)
    cc, sc = _trig(c, c, gw)
    wc = jnp.concatenate([cc, -sc], axis=1).astype(BF16)
    i1 = jnp.arange(s1, dtype=jnp.int32)
    i2 = jnp.arange(s2, dtype=jnp.int32)
    c1, sn1 = _trig(i1, i1, s1)
    m1 = jnp.concatenate([jnp.concatenate([c1, sn1], axis=1),
                          jnp.concatenate([-sn1, c1], axis=1)], axis=0).astype(BF16)
    c2, sn2 = _trig(i2, i2, s2)
    m2 = jnp.concatenate([c2, sn2], axis=1).astype(BF16)
    tw_c, tw_s = _trig(i1, i2, s1 * s2)
    return wc, m1, m2, tw_c[..., None], tw_s[..., None]


def _ffn_kernel(idx_cur, idx_nxt, gate_ref, x_hbm, w1_ref, w3_ref, w2_ref, ye_ref, xbuf, sem, *, tm, ffc):
    s = pl.program_id(0)
    ns = pl.num_programs(0)
    slot = s % 2

    def gather(idx_ref, sl):
        for r in range(tm):
            pltpu.make_async_copy(x_hbm.at[pl.ds(idx_ref[0, 0, r], 1), :], xbuf.at[sl, pl.ds(r, 1), :],
                                  sem.at[sl]).start()

    @pl.when(s == 0)
    def _():
        gather(idx_cur, 0)

    @pl.when((s + 1 < ns) & (slot == 0))
    def _():
        gather(idx_nxt, 1)

    @pl.when((s + 1 < ns) & (slot == 1))
    def _():
        gather(idx_nxt, 0)

    pltpu.make_async_copy(x_hbm.at[pl.ds(0, tm), :], xbuf.at[slot], sem.at[slot]).wait()
    xb = xbuf[slot].astype(BF16)
    dff = w1_ref.shape[1]
    acc = jnp.zeros(ye_ref.shape, F32)
    for c0 in range(0, dff, ffc):
        h1 = jnp.dot(xb, w1_ref[:, c0:c0 + ffc], preferred_element_type=F32)
        h3 = jnp.dot(xb, w3_ref[:, c0:c0 + ffc], preferred_element_type=F32)
        gg = (jax.nn.silu(h1) * h3).astype(BF16)
        acc = acc + jnp.dot(gg, w2_ref[c0:c0 + ffc, :], preferred_element_type=F32)
    ye_ref[...] = acc * gate_ref[...]


def _moe_ffn(x1, idx, gate, w1, w3, w2):
    n, d = x1.shape
    e, cap = idx.shape
    dff = w1.shape[2]
    tm = min(256, cap)
    nblk = cap // tm
    ns = e * nblk
    ffc = dff // 2 if (dff // 2) % V7X_LANES == 0 else dff
    idx3 = idx.reshape(ns, 1, tm)
    gate2 = gate.reshape(e * cap, 1)
    smem_blk = lambda f: pl.BlockSpec((1, 1, tm), f, memory_space=pltpu.SMEM)
    return pl.pallas_call(
        functools.partial(_ffn_kernel, tm=tm, ffc=ffc),
        grid=(ns,),
        in_specs=[
            smem_blk(lambda s: (s, 0, 0)),
            smem_blk(lambda s: (jnp.minimum(s + 1, ns - 1), 0, 0)),
            pl.BlockSpec((tm, 1), lambda s: (s, 0)),
            pl.BlockSpec(memory_space=pl.ANY),
            pl.BlockSpec((None, d, dff), lambda s: (s // nblk, 0, 0)),
            pl.BlockSpec((None, d, dff), lambda s: (s // nblk, 0, 0)),
            pl.BlockSpec((None, dff, d), lambda s: (s // nblk, 0, 0)),
        ],
        out_specs=pl.BlockSpec((tm, d), lambda s: (s, 0)),
        out_shape=jax.ShapeDtypeStruct((e * cap, d), F32),
        scratch_shapes=[pltpu.VMEM((2, tm, d), F32), pltpu.SemaphoreType.DMA((2,))],
        compiler_params=_cparams(("arbitrary",), 56),
        name="moe_ffn",
    )(idx3, idx3, gate2, x1, w1, w3, w2)


def _combine_kernel(tile_ref, chunk_ref, first_ref, last_ref, valid_ref,
                    slot_cur, slot_nxt, tok_ref, x_ref, ye_hbm, lg_ref, lb_ref, o_ref, acc_ref, buf, sem, *, tt, ca):
    w = pl.program_id(0)
    nw = pl.num_programs(0)
    par = w % 2

    def gather(slot_ref, sl):
        for r in range(ca):
            pltpu.make_async_copy(ye_hbm.at[pl.ds(slot_ref[0, 0, r], 1), :], buf.at[sl, pl.ds(r, 1), :],
                                  sem.at[sl]).start()

    @pl.when((w == 0) & (valid_ref[0] == 1))
    def _():
        gather(slot_cur, 0)

    nxt_valid = valid_ref[jnp.minimum(w + 1, nw - 1)] * (w + 1 < nw).astype(jnp.int32) == 1

    @pl.when(nxt_valid & (par == 0))
    def _():
        gather(slot_nxt, 1)

    @pl.when(nxt_valid & (par == 1))
    def _():
        gather(slot_nxt, 0)

    @pl.when(first_ref[w] == 1)
    def _():
        acc_ref[...] = jnp.zeros_like(acc_ref)

    @pl.when(valid_ref[w] == 1)
    def _():
        pltpu.make_async_copy(ye_hbm.at[pl.ds(0, ca), :], buf.at[par], sem.at[par]).wait()
        t0 = tile_ref[w] * tt
        row = lax.broadcasted_iota(jnp.int32, (tt, ca), 0) + t0
        onehot = jnp.where(tok_ref[0] == row, 1.0, 0.0).astype(BF16)
        acc_ref[...] += jnp.dot(onehot, buf[par].astype(BF16), preferred_element_type=F32)

    @pl.when(last_ref[w] == 1)
    def _():
        o_ref[...] = _layer_norm(DEEPNORM_ALPHA * x_ref[...] + acc_ref[...], lg_ref[...], lb_ref[...])


def _moe_combine(x1, ye, idx, lg, lb):
    n, d = x1.shape
    na = ye.shape[0]
    tt = min(256, n)
    ca = min(256, na)
    nt = n // tt
    nch = na // ca
    nw = nt + nch
    tok_flat = idx.reshape(-1)
    order = jnp.argsort(tok_flat).astype(jnp.int32)
    tok_sorted = tok_flat[order]
    bounds = jnp.searchsorted(tok_sorted, jnp.arange(nt + 1, dtype=jnp.int32) * tt, side='left').astype(jnp.int32)
    start, end = bounds[:-1], bounds[1:]
    c_lo = jnp.minimum(start // ca, nch - 1)
    c_hi = jnp.where(end > start, (end - 1) // ca, c_lo)
    cnt = c_hi - c_lo + 1
    off = jnp.cumsum(cnt) - cnt
    total = off[-1] + cnt[-1]
    wi = jnp.arange(nw, dtype=jnp.int32)
    tile_w = jnp.clip(jnp.searchsorted(off, wi, side='right').astype(jnp.int32) - 1, 0, nt - 1)
    valid = wi < total
    rel = wi - off[tile_w]
    chunk_w = jnp.where(valid, c_lo[tile_w] + rel, nch - 1).astype(jnp.int32)
    first = (valid & (rel == 0)).astype(jnp.int32)
    last = (valid & (rel == cnt[tile_w] - 1)).astype(jnp.int32)
    tile_w = jnp.where(valid, tile_w, nt - 1).astype(jnp.int32)
    valid = valid.astype(jnp.int32)
    slot3 = order.reshape(nch, 1, ca)
    tok3 = tok_sorted.reshape(nch, 1, ca)
    grid_spec = pltpu.PrefetchScalarGridSpec(
        num_scalar_prefetch=5,
        grid=(nw,),
        in_specs=[
            pl.BlockSpec((1, 1, ca), lambda w, tl, ch, fi, la, va: (ch[w], 0, 0), memory_space=pltpu.SMEM),
            pl.BlockSpec((1, 1, ca), lambda w, tl, ch, fi, la, va: (ch[jnp.minimum(w + 1, nw - 1)], 0, 0),
                         memory_space=pltpu.SMEM),
            pl.BlockSpec((1, 1, ca), lambda w, tl, ch, fi, la, va: (ch[w], 0, 0)),
            pl.BlockSpec((tt, d), lambda w, tl, ch, fi, la, va: (tl[w], 0)),
            pl.BlockSpec(memory_space=pl.ANY),
            pl.BlockSpec((1, d), lambda w, tl, ch, fi, la, va: (0, 0)),
            pl.BlockSpec((1, d), lambda w, tl, ch, fi, la, va: (0, 0)),
        ],
        out_specs=pl.BlockSpec((tt, d), lambda w, tl, ch, fi, la, va: (tl[w], 0)),
        scratch_shapes=[pltpu.VMEM((tt, d), F32), pltpu.VMEM((2, ca, d), F32), pltpu.SemaphoreType.DMA((2,))],
    )
    return pl.pallas_call(
        functools.partial(_combine_kernel, tt=tt, ca=ca),
        grid_spec=grid_spec,
        out_shape=jax.ShapeDtypeStruct((n, d), F32),
        compiler_params=_cparams(("arbitrary",), 32),
        name="moe_combine",
    )(tile_w, chunk_w, first, last, valid, slot3, slot3, tok3, x1, ye, lg, lb)


def _expert_choice(x1, aff_t, w1, w3, w2, lg, lb):
    ne, n = aff_t.shape
    cap = max(1, EC_CAPACITY_FACTOR * n // ne)
    gate, idx = lax.top_k(aff_t, cap)
    ye = _moe_ffn(x1, idx.astype(jnp.int32), gate, w1, w3, w2)
    return _moe_combine(x1, ye, idx.astype(jnp.int32), lg, lb)


def _rope_tables(seq, reps):
    dim = ATT_HEAD_DIM
    inv = 1.0 / (ROPE_THETA ** (jnp.arange(0, dim, 2, dtype=F32) / dim))
    ang = jnp.arange(seq, dtype=F32)[:, None] * inv[None, :]
    ang = jnp.concatenate([ang, ang], -1)
    return jnp.tile(jnp.cos(ang), (1, reps)), jnp.tile(jnp.sin(ang), (1, reps))


def _even_layer(x3, p, s5tabs_fn, lambda_init, l):
    b, s, d = x3.shape
    n = b * s
    x2 = x3.reshape(n, d)
    qk = p['qk']
    av = p['av']
    cos_t, sin_t = _rope_tables(s, 2 * qk // ATT_HEAD_DIM)
    q, k, v, u = _proj_in(x2, p['w_in'], p['w_rot'], cos_t, sin_t, s, qk, av)
    heads = av // ATT_V_DIM
    tk = min(512, s // 2)
    vt = v.reshape(b, s // tk, tk, heads, ATT_V_DIM).transpose(0, 3, 1, 4, 2)
    ot = _diff_attention(q.reshape(b, s, qk), k.reshape(b, s, qk), vt, p['lam'], p['subln_g'], lambda_init, tk)
    attn = ot.transpose(0, 3, 1, 2).reshape(n, av)
    L = min(S5_CHUNK, s)
    nc = s // L
    G, H = p['s5_g'], p['s5_h']
    ug = u.reshape(b, nc, L, G, H).transpose(3, 0, 1, 2, 4).reshape(G, b * nc, L * H)
    yg = _s5_scan(ug, s5tabs_fn(L, nc), nc)
    yg = yg.reshape(G, b, nc, L, H).transpose(1, 2, 3, 0, 4).reshape(n, G * H)
    return _mix_even(x2, attn, yg, p['glu_w'], p['glu_b'], p['w_out_even'], p['ln_mix_g'][l], p['ln_mix_b'][l],
                     p['w_router'][l])


def _odd_layer(x3, p, l):
    b, s, d = x3.shape
    gw = d // FNET_GROUPS
    s2 = 128 if s >= 1024 else s // 8
    s1 = s // s2
    wc, m1, m2, tw_c, tw_s = _dft_tables(s1, s2, gw)
    z = _chan_dft(x3, wc).reshape(b, 2 * s1, s2 * d)
    g = _dft_stage1(m1, z).reshape(b, 2, s1, s2, d)
    f = _dft_stage2(m2, g, tw_c, tw_s, 1.0 / math.sqrt(s * gw))
    return _mix_odd(x3.reshape(b * s, d), f.reshape(b * s, d), p['w_out_odd'], p['ln_mix_g'][l], p['ln_mix_b'][l],
                    p['w_router'][l])


def _trunk(x3, p, s5tabs_fn):
    b, s, d = x3.shape
    for l in range(DEPTH):
        if l % 2 == 0:
            lambda_init = 0.8 - 0.6 * math.exp(-0.3 * l)
            x1, aff = _even_layer(x3, p, s5tabs_fn, lambda_init, l)
        else:
            x1, aff = _odd_layer(x3, p, l)
        x2 = _expert_choice(x1, aff, p['w_ff1'][l], p['w_ff3'][l], p['w_ff2'][l], p['ln_ffn_g'][l], p['ln_ffn_b'][l])
        x3 = x2.reshape(b, s, d)
    return x3


def kernel(x_prompt, x_sample, w_in, lam_q1, lam_k1, lam_q2, lam_k2, subln_g, s5_a_re, s5_a_im, s5_log_dt, s5_b_re, s5_b_im, s5_c_re, s5_c_im, s5_d, s5_glu_w, s5_glu_b, w_out_even, w_out_odd, ln_mix_g, ln_mix_b, w_router, w_ff1, w_ff3, w_ff2, ln_ffn_g, ln_ffn_b):
    assert w_in.shape[0] == 1 and w_out_odd.shape[0] == 1 and DEPTH == 2
    d = w_in.shape[1]
    av = 4 * ATT_V_DIM
    qk = av
    wi = w_in[0]
    wqk = wi[:, :2 * qk].reshape(d, 2 * qk // ATT_HEAD_DIM, 2, ATT_HEAD_DIM // 2)
    w_rot = jnp.stack([-wqk[:, :, 1], wqk[:, :, 0]], axis=2).reshape(d, 2 * qk)
    lam = (jnp.exp(jnp.sum(lam_q1[0] * lam_k1[0])) - jnp.exp(jnp.sum(lam_q2[0] * lam_k2[0]))
           + (0.8 - 0.6 * math.exp(0.0))).reshape(1).astype(F32)
    p = {
        'qk': qk, 'av': av, 's5_g': s5_b_re.shape[2], 's5_h': s5_b_re.shape[4],
        'w_in': wi.astype(BF16), 'w_rot': w_rot.astype(BF16), 'lam': lam,
        'subln_g': subln_g[0].reshape(-1, 1),
        'glu_w': s5_glu_w[0].astype(BF16), 'glu_b': s5_glu_b[0].reshape(1, -1),
        'w_out_even': w_out_even[0].astype(BF16), 'w_out_odd': w_out_odd[0].astype(BF16),
        'ln_mix_g': ln_mix_g[:, None, :], 'ln_mix_b': ln_mix_b[:, None, :],
        'ln_ffn_g': ln_ffn_g[:, None, :], 'ln_ffn_b': ln_ffn_b[:, None, :],
        'w_router': w_router.transpose(0, 2, 1),
        'w_ff1': w_ff1.astype(BF16), 'w_ff3': w_ff3.astype(BF16), 'w_ff2': w_ff2.astype(BF16),
    }

    def s5tabs_fn(chunk, nc):
        nsteps = max(1, (nc - 1).bit_length())
        return _s5_tables(s5_a_re[0], s5_a_im[0], s5_log_dt[0], s5_b_re[0], s5_b_im[0], s5_c_re[0], s5_c_im[0],
                          s5_d[0], chunk, nsteps)

    return (_trunk(x_prompt, p, s5tabs_fn), _trunk(x_sample, p, s5tabs_fn))
```

```python
import functools
import math

import jax
import jax.numpy as jnp
from jax import lax
from jax.experimental import pallas as pl
from jax.experimental.pallas import tpu as pltpu

F32 = jnp.float32
BF16 = jnp.bfloat16
HI = lax.Precision.HIGHEST

ATT_HEAD_DIM = 64
ATT_V_DIM = 2 * ATT_HEAD_DIM
FNET_GROUPS = 4
EC_CAPACITY_FACTOR = 2
ROPE_THETA = 10000.0
LN_EPS = 1e-5
DEPTH = 2
DEEPNORM_ALPHA = (2 * DEPTH) ** 0.25

V7X_VMEM_BYTES = 64 * 1024 * 1024
V7X_LANES = 128

S5_CHUNK = 64


def _cparams(semantics, vmem_mib):
    return pltpu.CompilerParams(dimension_semantics=semantics,
                                vmem_limit_bytes=min(vmem_mib * 1024 * 1024, V7X_VMEM_BYTES * 7 // 8))


def _row_parts(rows, parts=2):
    step = rows // parts if rows % (parts * 2 * V7X_LANES) == 0 else rows
    return [slice(r0, r0 + step) for r0 in range(0, rows, step)]


def _layer_norm(z, g, b):
    mu = jnp.mean(z, -1, keepdims=True)
    zc = z - mu
    var = jnp.mean(zc * zc, -1, keepdims=True)
    return zc * lax.rsqrt(var + LN_EPS) * g + b


def _proj_in_kernel(x_ref, w_ref, wrot_ref, cos_ref, sin_ref, q_ref, k_ref, v_ref, u_ref, *, qk, av, scale):
    for rows in _row_parts(x_ref.shape[0]):
        xb = x_ref[rows, :].astype(BF16)
        h = jnp.dot(xb, w_ref[...], preferred_element_type=F32)
        hr = jnp.dot(xb, wrot_ref[...], preferred_element_type=F32)
        roped = h[:, :2 * qk] * cos_ref[rows, :] + hr * sin_ref[rows, :]
        q_ref[rows, :] = (roped[:, :qk] * scale).astype(BF16)
        k_ref[rows, :] = roped[:, qk:].astype(BF16)
        v_ref[rows, :] = h[:, 2 * qk:2 * qk + av].astype(BF16)
        u_ref[rows, :] = h[:, 2 * qk + av:].astype(BF16)


def _proj_in(x2, w, wrot, cos_t, sin_t, seq, qk, av):
    n, d = x2.shape
    sw = w.shape[1] - 2 * qk - av
    tm = min(512, seq)
    nsb = seq // tm
    kern = functools.partial(_proj_in_kernel, qk=qk, av=av, scale=ATT_HEAD_DIM ** -0.5 * math.log2(math.e))
    return pl.pallas_call(
        kern,
        grid=(n // tm,),
        in_specs=[
            pl.BlockSpec((tm, d), lambda i: (i, 0)),
            pl.BlockSpec(w.shape, lambda i: (0, 0)),
            pl.BlockSpec(wrot.shape, lambda i: (0, 0)),
            pl.BlockSpec((tm, 2 * qk), lambda i: (i % nsb, 0)),
            pl.BlockSpec((tm, 2 * qk), lambda i: (i % nsb, 0)),
        ],
        out_specs=[
            pl.BlockSpec((tm, qk), lambda i: (i, 0)),
            pl.BlockSpec((tm, qk), lambda i: (i, 0)),
            pl.BlockSpec((tm, av), lambda i: (i, 0)),
            pl.BlockSpec((tm, sw), lambda i: (i, 0)),
        ],
        out_shape=[
            jax.ShapeDtypeStruct((n, qk), BF16),
            jax.ShapeDtypeStruct((n, qk), BF16),
            jax.ShapeDtypeStruct((n, av), BF16),
            jax.ShapeDtypeStruct((n, sw), BF16),
        ],
        compiler_params=_cparams(("parallel",), 48),
        name="proj_in",
    )(x2, w, wrot, cos_t, sin_t)


def _attn_kernel(lam_ref, q_ref, k_ref, vt_ref, g_ref, o_ref, acc1, acc2, sa1, sa2, sb1, sb2, *, tk, nkv,
                 out_scale):
    q = q_ref[...]
    tq = q.shape[0]
    lane = lax.broadcasted_iota(jnp.int32, q.shape, 1)
    zero = jnp.zeros_like(q)
    q1 = jnp.where(lane < ATT_HEAD_DIM, q, zero)
    q2 = jnp.where(lane >= ATT_HEAD_DIM, q, zero)
    acc1[...] = jnp.zeros_like(acc1)
    acc2[...] = jnp.zeros_like(acc2)
    nt = (((1,), (1,)), ((), ()))

    def scores(j, d1, d2):
        kb = k_ref[pl.ds(pl.multiple_of(j * tk, tk), tk), :]
        d1[...] = lax.dot_general(kb, q1, nt, preferred_element_type=F32)
        d2[...] = lax.dot_general(kb, q2, nt, preferred_element_type=F32)

    def soft_pv(vb, s_ref, m, l, acc):
        s = s_ref[...]
        mn = jnp.maximum(m, jnp.max(s, axis=0, keepdims=True))
        p = jnp.exp2(s - mn)
        a = jnp.exp2(m - mn)
        l = a * l + jnp.sum(p, axis=0, keepdims=True)
        acc[...] = a * acc[...] + jnp.dot(vb, p.astype(BF16), preferred_element_type=F32)
        return mn, l

    def step(j, cur, nxt, carry, prefetch):
        m1, l1, m2, l2 = carry
        if prefetch:
            scores(j + 1, *nxt)
        vb = vt_ref[j]
        m1, l1 = soft_pv(vb, cur[0], m1, l1, acc1)
        m2, l2 = soft_pv(vb, cur[1], m2, l2, acc2)
        return m1, l1, m2, l2

    buf_a, buf_b = (sa1, sa2), (sb1, sb2)

    def body(i, carry):
        carry = step(2 * i, buf_a, buf_b, carry, True)
        return step(2 * i + 1, buf_b, buf_a, carry, True)

    neg = jnp.full((1, tq), -1e30, F32)
    zer = jnp.zeros((1, tq), F32)
    scores(0, *buf_a)
    carry = lax.fori_loop(0, nkv // 2 - 1, body, (neg, zer, neg, zer))
    carry = step(nkv - 2, buf_a, buf_b, carry, True)
    _, l1, _, l2 = step(nkv - 1, buf_b, buf_a, carry, False)
    o = acc1[...] / l1 - lam_ref[0] * (acc2[...] / l2)
    ms = jnp.mean(o * o, axis=0, keepdims=True)
    o = o * lax.rsqrt(ms + LN_EPS) * g_ref[...] * out_scale
    o_ref[...] = o.astype(o_ref.dtype)


def _diff_attention(q, k, vt, lam, g_col, lambda_init, tk):
    b, s, _ = q.shape
    heads = vt.shape[1]
    nkv = vt.shape[2]
    assert nkv % 2 == 0
    dv = vt.shape[3]
    tq = min(1024, s)
    kern = functools.partial(_attn_kernel, tk=tk, nkv=nkv, out_scale=1.0 - lambda_init)
    return pl.pallas_call(
        kern,
        grid=(b, heads, s // tq),
        in_specs=[
            pl.BlockSpec(memory_space=pltpu.SMEM),
            pl.BlockSpec((None, tq, 2 * ATT_HEAD_DIM), lambda bi, hi, qi: (bi, qi, hi)),
            pl.BlockSpec((None, s, 2 * ATT_HEAD_DIM), lambda bi, hi, qi: (bi, 0, hi)),
            pl.BlockSpec((None, None, nkv, dv, tk), lambda bi, hi, qi: (bi, hi, 0, 0, 0)),
            pl.BlockSpec((dv, 1), lambda bi, hi, qi: (0, 0)),
        ],
        out_specs=pl.BlockSpec((None, None, dv, tq), lambda bi, hi, qi: (bi, hi, 0, qi)),
        out_shape=jax.ShapeDtypeStruct((b, heads, dv, s), BF16),
        scratch_shapes=[pltpu.VMEM((dv, tq), F32)] * 2 + [pltpu.VMEM((tk, tq), F32)] * 4,
        compiler_params=_cparams(("parallel", "parallel", "arbitrary"), 48),
        name="diff_attn",
    )(lam, q, k, vt, g_col)


def _s5_kernel(u_ref, t_ref, we_ref, ws_ref, a1_ref, a2_ref, y_ref, *, nc, nsteps, p2):
    u = u_ref[...]
    r = u.shape[0]
    e = jnp.dot(u, we_ref[...], preferred_element_type=F32)
    c = lax.broadcasted_iota(jnp.int32, (r, p2), 0) % nc
    half = p2 // 2
    hf = e[:, :p2]
    hb = e[:, p2:]
    for kk in range(nsteps):
        sh = 1 << kk
        pf = jnp.where(c >= sh, pltpu.roll(hf, sh, axis=0), 0.0)
        hf = hf + a1_ref[kk:kk + 1, :p2] * pf + a2_ref[kk:kk + 1, :p2] * pltpu.roll(pf, half, axis=1)
        pb = jnp.where(c < nc - sh, pltpu.roll(hb, r - sh, axis=0), 0.0)
        hb = hb + a1_ref[kk:kk + 1, p2:] * pb + a2_ref[kk:kk + 1, p2:] * pltpu.roll(pb, half, axis=1)
    hf_in = jnp.where(c >= 1, pltpu.roll(hf, 1, axis=0), 0.0)
    hb_in = jnp.where(c < nc - 1, pltpu.roll(hb, r - 1, axis=0), 0.0)
    hin = jnp.concatenate([hf_in, hb_in], axis=1).astype(BF16)
    y = jnp.dot(u, t_ref[...], preferred_element_type=F32)
    y = y + jnp.dot(hin, ws_ref[...], preferred_element_type=F32)
    y_ref[...] = jax.nn.gelu(y).astype(BF16)


def _s5_scan(ug, tabs, nc):
    t_tot, we, ws, a1, a2 = tabs
    g, r, lh = ug.shape
    p4 = we.shape[2]
    nsteps = a1.shape[1]
    kern = functools.partial(_s5_kernel, nc=nc, nsteps=nsteps, p2=p4 // 2)
    return pl.pallas_call(
        kern,
        grid=(g,),
        in_specs=[
            pl.BlockSpec((None, r, lh), lambda i: (i, 0, 0)),
            pl.BlockSpec((None, lh, lh), lambda i: (i, 0, 0)),
            pl.BlockSpec((None, lh, p4), lambda i: (i, 0, 0)),
            pl.BlockSpec((None, p4, lh), lambda i: (i, 0, 0)),
            pl.BlockSpec((None, nsteps, p4), lambda i: (i, 0, 0)),
            pl.BlockSpec((None, nsteps, p4), lambda i: (i, 0, 0)),
        ],
        out_specs=pl.BlockSpec((None, r, lh), lambda i: (i, 0, 0)),
        out_shape=jax.ShapeDtypeStruct((g, r, lh), BF16),
        compiler_params=_cparams(("parallel",), 48),
        name="s5_scan",
    )(ug, t_tot, we, ws, a1, a2)


def _s5_tables(a_re, a_im, log_dt, b_re, b_im, c_re, c_im, d, chunk, nsteps):
    L = chunk
    G, P = a_re.shape[1], a_re.shape[2]
    H = b_re.shape[3]
    dt = jnp.exp(log_dt)[..., None]
    mag = jnp.exp(a_re * dt)
    lr = mag * jnp.cos(a_im * dt)
    li = mag * jnp.sin(a_im * dt)
    nr = lr - 1.0
    den = a_re * a_re + a_im * a_im
    cr = (nr * a_re + li * a_im) / den
    ci = (li * a_re - nr * a_im) / den
    bbr = cr[..., None] * b_re - ci[..., None] * b_im
    bbi = cr[..., None] * b_im + ci[..., None] * b_re
    j = jnp.arange(L + 1, dtype=F32)[:, None, None, None]
    pm = jnp.exp(j * (a_re * dt)[None])
    ang = j * (a_im * dt)[None]
    pr = pm * jnp.cos(ang)
    pi = pm * jnp.sin(ang)
    cbr = c_re[..., None] * bbr[:, :, None] - c_im[..., None] * bbi[:, :, None]
    cbi = c_re[..., None] * bbi[:, :, None] + c_im[..., None] * bbr[:, :, None]
    kern = (jnp.einsum('jdgp,dghpk->dgjhk', pr[:L], cbr, precision=HI)
            - jnp.einsum('jdgp,dghpk->dgjhk', pi[:L], cbi, precision=HI))
    s_i = jnp.arange(L)[:, None]
    t_i = jnp.arange(L)[None, :]
    lag_f = jnp.clip(t_i - s_i, 0, L - 1)
    lag_b = jnp.clip(s_i - t_i, 0, L - 1)
    tf = kern[0][:, lag_f] * (t_i >= s_i)[None, :, :, None, None]
    tb = kern[1][:, lag_b] * (s_i >= t_i)[None, :, :, None, None]
    skip = (jnp.eye(L, dtype=F32)[None, :, :, None, None] * jnp.eye(H, dtype=F32)[None, None, None]
            * d[:, None, None, :, None])
    t_tot = (tf + tb + skip).transpose(0, 1, 4, 2, 3).reshape(G, L * H, L * H)
    def state_in(pw_r, pw_i, br, bi):
        re = pw_r[..., None] * br[None] - pw_i[..., None] * bi[None]
        im = pw_r[..., None] * bi[None] + pw_i[..., None] * br[None]
        both = jnp.concatenate([re, im], axis=2)
        return both.transpose(1, 0, 3, 2).reshape(G, L * H, 2 * P)
    we = jnp.concatenate([state_in(pr[:L][::-1, 0], pi[:L][::-1, 0], bbr[0], bbi[0]),
                          state_in(pr[:L, 1], pi[:L, 1], bbr[1], bbi[1])], axis=2)
    def state_out(pw_r, pw_i, cre, cim):
        re = cre[None] * pw_r[:, :, None, :] - cim[None] * pw_i[:, :, None, :]
        im = cre[None] * pw_i[:, :, None, :] + cim[None] * pw_r[:, :, None, :]
        both = jnp.concatenate([re, -im], axis=3)
        return both.transpose(1, 3, 0, 2).reshape(G, 2 * P, L * H)
    ws = jnp.concatenate([state_out(pr[1:, 0], pi[1:, 0], c_re[0], c_im[0]),
                          state_out(pr[1:][::-1, 1], pi[1:][::-1, 1], c_re[1], c_im[1])], axis=1)
    ar, ai = pr[L], pi[L]
    a1, a2 = [], []
    for _ in range(nsteps):
        a1.append(jnp.concatenate([ar[0], ar[0], ar[1], ar[1]], axis=-1))
        a2.append(jnp.concatenate([-ai[0], ai[0], -ai[1], ai[1]], axis=-1))
        ar, ai = ar * ar - ai * ai, 2.0 * ar * ai
    a1 = jnp.stack(a1, axis=1)
    a2 = jnp.stack(a2, axis=1)
    return t_tot.astype(BF16), we.astype(BF16), ws.astype(BF16), a1, a2


def _router_t(x1, wrt_ref):
    logits = lax.dot_general(wrt_ref[...], x1, (((1,), (1,)), ((), ())), preferred_element_type=F32, precision=HI)
    logits = logits - jnp.max(logits, axis=0, keepdims=True)
    ex = jnp.exp(logits)
    return ex / jnp.sum(ex, axis=0, keepdims=True)


def _mix_even_kernel(x_ref, attn_ref, y_ref, gw_ref, gb_ref, wo_ref, lg_ref, lb_ref, wrt_ref,
                     x1_ref, aff_ref, *, aw):
    for rows in _row_parts(x_ref.shape[0]):
        y = y_ref[rows, :]
        gate = jax.nn.sigmoid(jnp.dot(y, gw_ref[...], preferred_element_type=F32) + gb_ref[...])
        ssm = (y.astype(F32) * gate).astype(BF16)
        m = jnp.dot(attn_ref[rows, :], wo_ref[:aw, :], preferred_element_type=F32)
        m = m + jnp.dot(ssm, wo_ref[aw:, :], preferred_element_type=F32)
        x1 = _layer_norm(DEEPNORM_ALPHA * x_ref[rows, :] + m, lg_ref[...], lb_ref[...])
        x1_ref[rows, :] = x1
        aff_ref[:, rows] = _router_t(x1, wrt_ref)


def _mix_odd_kernel(x_ref, f_ref, wo_ref, lg_ref, lb_ref, wrt_ref, x1_ref, aff_ref):
    for rows in _row_parts(x_ref.shape[0]):
        m = jnp.dot(f_ref[rows, :].astype(BF16), wo_ref[...], preferred_element_type=F32)
        x1 = _layer_norm(DEEPNORM_ALPHA * x_ref[rows, :] + m, lg_ref[...], lb_ref[...])
        x1_ref[rows, :] = x1
        aff_ref[:, rows] = _router_t(x1, wrt_ref)


def _full(a):
    nd = a.ndim
    return pl.BlockSpec(a.shape, lambda i: (0,) * nd)


def _rows(tm, width):
    return pl.BlockSpec((tm, width), lambda i: (i, 0))


def _mix_out(n, d, ne, tm):
    specs = [pl.BlockSpec((tm, d), lambda i: (i, 0)), pl.BlockSpec((ne, tm), lambda i: (0, i))]
    shapes = [jax.ShapeDtypeStruct((n, d), F32), jax.ShapeDtypeStruct((ne, n), F32)]
    return specs, shapes


def _mix_even(x2, attn, yg, gw, gb, wo, lg, lb, wrt):
    n, d = x2.shape
    tm = min(512, n)
    aw = attn.shape[1]
    out_specs, out_shape = _mix_out(n, d, wrt.shape[0], tm)
    return pl.pallas_call(
        functools.partial(_mix_even_kernel, aw=aw),
        grid=(n // tm,),
        in_specs=[_rows(tm, d), _rows(tm, aw), _rows(tm, yg.shape[1]), _full(gw), _full(gb), _full(wo),
                  _full(lg), _full(lb), _full(wrt)],
        out_specs=out_specs,
        out_shape=out_shape,
        compiler_params=_cparams(("parallel",), 48),
        name="mix_even",
    )(x2, attn, yg, gw, gb, wo, lg, lb, wrt)


def _mix_odd(x2, f, wo, lg, lb, wrt):
    n, d = x2.shape
    tm = min(512, n)
    out_specs, out_shape = _mix_out(n, d, wrt.shape[0], tm)
    return pl.pallas_call(
        _mix_odd_kernel,
        grid=(n // tm,),
        in_specs=[_rows(tm, d), _rows(tm, d), _full(wo), _full(lg), _full(lb), _full(wrt)],
        out_specs=out_specs,
        out_shape=out_shape,
        compiler_params=_cparams(("parallel",), 48),
        name="mix_odd",
    )(x2, f, wo, lg, lb, wrt)


def _chan_dft_kernel(x_ref, w_ref, z_ref, *, groups, gw):
    x = x_ref[...].astype(BF16)
    w = w_ref[...]
    for gi in range(groups):
        z = jnp.dot(x[:, gi * gw:(gi + 1) * gw], w, preferred_element_type=F32)
        z_ref[0, :, gi * gw:(gi + 1) * gw] = z[:, :gw].astype(BF16)
        z_ref[1, :, gi * gw:(gi + 1) * gw] = z[:, gw:].astype(BF16)


def _chan_dft(x3, w):
    b, s, d = x3.shape
    tm = min(512, s)
    gw = d // FNET_GROUPS
    return pl.pallas_call(
        functools.partial(_chan_dft_kernel, groups=FNET_GROUPS, gw=gw),
        grid=(b, s // tm),
        in_specs=[pl.BlockSpec((None, tm, d), lambda bi, i: (bi, i, 0)),
                  pl.BlockSpec(w.shape, lambda bi, i: (0, 0))],
        out_specs=pl.BlockSpec((None, 2, tm, d), lambda bi, i: (bi, 0, i, 0)),
        out_shape=jax.ShapeDtypeStruct((b, 2, s, d), BF16),
        compiler_params=_cparams(("parallel", "parallel"), 32),
        name="chan_dft",
    )(x3, w)


def _dft_stage1_kernel(m_ref, z_ref, g_ref):
    g_ref[...] = jnp.dot(m_ref[...], z_ref[...], preferred_element_type=F32).astype(BF16)


def _dft_stage1(m1, z):
    b, r, cols = z.shape
    tn = min(8192, cols)
    return pl.pallas_call(
        _dft_stage1_kernel,
        grid=(b, cols // tn),
        in_specs=[pl.BlockSpec(m1.shape, lambda bi, j: (0, 0)),
                  pl.BlockSpec((None, r, tn), lambda bi, j: (bi, 0, j))],
        out_specs=pl.BlockSpec((None, r, tn), lambda bi, j: (bi, 0, j)),
        out_shape=jax.ShapeDtypeStruct((b, r, cols), BF16),
        compiler_params=_cparams(("parallel", "parallel"), 32),
        name="dft_stage1",
    )(m1, z)


def _dft_stage2_kernel(m_ref, g_ref, tc_ref, ts_ref, o_ref, *, kb, d, scale):
    m = m_ref[...]
    for kk in range(kb):
        gr = g_ref[0, kk].astype(F32)
        gi = g_ref[1, kk].astype(F32)
        c = tc_ref[kk]
        s = ts_ref[kk]
        rhs = jnp.concatenate([gr * c + gi * s, gi * c - gr * s], axis=0).astype(BF16)
        o_ref[:, kk * d:(kk + 1) * d] = jnp.dot(m, rhs, preferred_element_type=F32) * scale


def _dft_stage2(m2, g5, tw_c, tw_s, scale):
    b, _, s1, s2, d = g5.shape
    kb = min(8, s1)
    return pl.pallas_call(
        functools.partial(_dft_stage2_kernel, kb=kb, d=d, scale=scale),
        grid=(b, s1 // kb),
        in_specs=[pl.BlockSpec(m2.shape, lambda bi, i: (0, 0)),
                  pl.BlockSpec((None, 2, kb, s2, d), lambda bi, i: (bi, 0, i, 0, 0)),
                  pl.BlockSpec((kb, s2, 1), lambda bi, i: (i, 0, 0)),
                  pl.BlockSpec((kb, s2, 1), lambda bi, i: (i, 0, 0))],
        out_specs=pl.BlockSpec((None, s2, kb * d), lambda bi, i: (bi, 0, i)),
        out_shape=jax.ShapeDtypeStruct((b, s2, s1 * d), F32),
        compiler_params=_cparams(("parallel", "parallel"), 48),
        name="dft_stage2",
    )(m2, g5, tw_c, tw_s)


def _trig(rows, cols, period):
    ang = (2.0 * math.pi / period) * ((rows[:, None] * cols[None, :]) % period).astype(F32)
    return jnp.cos(ang), jnp.sin(ang)


def _dft_tables(s1, s2, gw):
    c = jnp.arange(gw, dtype=jnp.int32)
    cc, sc = _trig(c, c, gw)
    wc = jnp.concatenate([cc, -sc], axis=1).astype(BF16)
    i1 = jnp.arange(s1, dtype=jnp.int32)
    i2 = jnp.arange(s2, dtype=jnp.int32)
    c1, sn1 = _trig(i1, i1, s1)
    m1 = jnp.concatenate([jnp.concatenate([c1, sn1], axis=1),
                          jnp.concatenate([-sn1, c1], axis=1)], axis=0).astype(BF16)
    c2, sn2 = _trig(i2, i2, s2)
    m2 = jnp.concatenate([c2, sn2], axis=1).astype(BF16)
    tw_c, tw_s = _trig(i1, i2, s1 * s2)
    return wc, m1, m2, tw_c[..., None], tw_s[..., None]


def _ffn_kernel(idx_cur, idx_nxt, gate_ref, x_hbm, w1_ref, w3_ref, w2_ref, ye_ref, xbuf, sem, *, tm, ffc):
    s = pl.program_id(0)
    ns = pl.num_programs(0)
    slot = s % 2

    def gather(idx_ref, sl):
        for r in range(tm):
            pltpu.make_async_copy(x_hbm.at[pl.ds(idx_ref[0, 0, r], 1), :], xbuf.at[sl, pl.ds(r, 1), :],
                                  sem.at[sl]).start()

    @pl.when(s == 0)
    def _():
        gather(idx_cur, 0)

    @pl.when((s + 1 < ns) & (slot == 0))
    def _():
        gather(idx_nxt, 1)

    @pl.when((s + 1 < ns) & (slot == 1))
    def _():
        gather(idx_nxt, 0)

    pltpu.make_async_copy(x_hbm.at[pl.ds(0, tm), :], xbuf.at[slot], sem.at[slot]).wait()
    xb = xbuf[slot].astype(BF16)
    dff = w1_ref.shape[1]
    acc = jnp.zeros(ye_ref.shape, F32)
    for c0 in range(0, dff, ffc):
        h1 = jnp.dot(xb, w1_ref[:, c0:c0 + ffc], preferred_element_type=F32)
        h3 = jnp.dot(xb, w3_ref[:, c0:c0 + ffc], preferred_element_type=F32)
        gg = (jax.nn.silu(h1) * h3).astype(BF16)
        acc = acc + jnp.dot(gg, w2_ref[c0:c0 + ffc, :], preferred_element_type=F32)
    ye_ref[...] = acc * gate_ref[...]


def _moe_ffn(x1, idx, gate, w1, w3, w2):
    n, d = x1.shape
    e, cap = idx.shape
    dff = w1.shape[2]
    tm = min(256, cap)
    nblk = cap // tm
    ns = e * nblk
    ffc = dff // 2 if (dff // 2) % V7X_LANES == 0 else dff
    idx3 = idx.reshape(ns, 1, tm)
    gate2 = gate.reshape(e * cap, 1)
    smem_blk = lambda f: pl.BlockSpec((1, 1, tm), f, memory_space=pltpu.SMEM)
    return pl.pallas_call(
        functools.partial(_ffn_kernel, tm=tm, ffc=ffc),
        grid=(ns,),
        in_specs=[
            smem_blk(lambda s: (s, 0, 0)),
            smem_blk(lambda s: (jnp.minimum(s + 1, ns - 1), 0, 0)),
            pl.BlockSpec((tm, 1), lambda s: (s, 0)),
            pl.BlockSpec(memory_space=pl.ANY),
            pl.BlockSpec((None, d, dff), lambda s: (s // nblk, 0, 0)),
            pl.BlockSpec((None, d, dff), lambda s: (s // nblk, 0, 0)),
            pl.BlockSpec((None, dff, d), lambda s: (s // nblk, 0, 0)),
        ],
        out_specs=pl.BlockSpec((tm, d), lambda s: (s, 0)),
        out_shape=jax.ShapeDtypeStruct((e * cap, d), F32),
        scratch_shapes=[pltpu.VMEM((2, tm, d), F32), pltpu.SemaphoreType.DMA((2,))],
        compiler_params=_cparams(("arbitrary",), 56),
        name="moe_ffn",
    )(idx3, idx3, gate2, x1, w1, w3, w2)


def _combine_kernel(tile_ref, chunk_ref, first_ref, last_ref, valid_ref,
                    slot_cur, slot_nxt, tok_ref, x_ref, ye_hbm, lg_ref, lb_ref, o_ref, acc_ref, buf, sem, *, tt, ca):
    w = pl.program_id(0)
    nw = pl.num_programs(0)
    par = w % 2

    def gather(slot_ref, sl):
        for r in range(ca):
            pltpu.make_async_copy(ye_hbm.at[pl.ds(slot_ref[0, 0, r], 1), :], buf.at[sl, pl.ds(r, 1), :],
                                  sem.at[sl]).start()

    @pl.when((w == 0) & (valid_ref[0] == 1))
    def _():
        gather(slot_cur, 0)

    nxt_valid = valid_ref[jnp.minimum(w + 1, nw - 1)] * (w + 1 < nw).astype(jnp.int32) == 1

    @pl.when(nxt_valid & (par == 0))
    def _():
        gather(slot_nxt, 1)

    @pl.when(nxt_valid & (par == 1))
    def _():
        gather(slot_nxt, 0)

    @pl.when(first_ref[w] == 1)
    def _():
        acc_ref[...] = jnp.zeros_like(acc_ref)

    @pl.when(valid_ref[w] == 1)
    def _():
        pltpu.make_async_copy(ye_hbm.at[pl.ds(0, ca), :], buf.at[par], sem.at[par]).wait()
        t0 = tile_ref[w] * tt
        row = lax.broadcasted_iota(jnp.int32, (tt, ca), 0) + t0
        onehot = jnp.where(tok_ref[0] == row, 1.0, 0.0).astype(BF16)
        acc_ref[...] += jnp.dot(onehot, buf[par].astype(BF16), preferred_element_type=F32)

    @pl.when(last_ref[w] == 1)
    def _():
        o_ref[...] = _layer_norm(DEEPNORM_ALPHA * x_ref[...] + acc_ref[...], lg_ref[...], lb_ref[...])


def _moe_combine(x1, ye, idx, lg, lb):
    n, d = x1.shape
    na = ye.shape[0]
    tt = min(256, n)
    ca = min(256, na)
    nt = n // tt
    nch = na // ca
    nw = nt + nch
    tok_flat = idx.reshape(-1)
    tok_sorted, order = lax.sort((tok_flat, jnp.arange(na, dtype=jnp.int32)), dimension=0, num_keys=1,
                                 is_stable=False)
    bounds = jnp.searchsorted(tok_sorted, jnp.arange(nt + 1, dtype=jnp.int32) * tt, side='left').astype(jnp.int32)
    start, end = bounds[:-1], bounds[1:]
    c_lo = jnp.minimum(start // ca, nch - 1)
    c_hi = jnp.where(end > start, (end - 1) // ca, c_lo)
    cnt = c_hi - c_lo + 1
    off = jnp.cumsum(cnt) - cnt
    total = off[-1] + cnt[-1]
    wi = jnp.arange(nw, dtype=jnp.int32)
    tile_w = jnp.clip(jnp.searchsorted(off, wi, side='right').astype(jnp.int32) - 1, 0, nt - 1)
    valid = wi < total
    rel = wi - off[tile_w]
    chunk_w = jnp.where(valid, c_lo[tile_w] + rel, nch - 1).astype(jnp.int32)
    first = (valid & (rel == 0)).astype(jnp.int32)
    last = (valid & (rel == cnt[tile_w] - 1)).astype(jnp.int32)
    tile_w = jnp.where(valid, tile_w, nt - 1).astype(jnp.int32)
    valid = valid.astype(jnp.int32)
    slot3 = order.reshape(nch, 1, ca)
    tok3 = tok_sorted.reshape(nch, 1, ca)
    grid_spec = pltpu.PrefetchScalarGridSpec(
        num_scalar_prefetch=5,
        grid=(nw,),
        in_specs=[
            pl.BlockSpec((1, 1, ca), lambda w, tl, ch, fi, la, va: (ch[w], 0, 0), memory_space=pltpu.SMEM),
            pl.BlockSpec((1, 1, ca), lambda w, tl, ch, fi, la, va: (ch[jnp.minimum(w + 1, nw - 1)], 0, 0),
                         memory_space=pltpu.SMEM),
            pl.BlockSpec((1, 1, ca), lambda w, tl, ch, fi, la, va: (ch[w], 0, 0)),
            pl.BlockSpec((tt, d), lambda w, tl, ch, fi, la, va: (tl[w], 0)),
            pl.BlockSpec(memory_space=pl.ANY),
            pl.BlockSpec((1, d), lambda w, tl, ch, fi, la, va: (0, 0)),
            pl.BlockSpec((1, d), lambda w, tl, ch, fi, la, va: (0, 0)),
        ],
        out_specs=pl.BlockSpec((tt, d), lambda w, tl, ch, fi, la, va: (tl[w], 0)),
        scratch_shapes=[pltpu.VMEM((tt, d), F32), pltpu.VMEM((2, ca, d), F32), pltpu.SemaphoreType.DMA((2,))],
    )
    return pl.pallas_call(
        functools.partial(_combine_kernel, tt=tt, ca=ca),
        grid_spec=grid_spec,
        out_shape=jax.ShapeDtypeStruct((n, d), F32),
        compiler_params=_cparams(("arbitrary",), 32),
        name="moe_combine",
    )(tile_w, chunk_w, first, last, valid, slot3, slot3, tok3, x1, ye, lg, lb)


def _expert_choice(x1, aff_t, w1, w3, w2, lg, lb):
    ne, n = aff_t.shape
    cap = max(1, EC_CAPACITY_FACTOR * n // ne)
    gate, idx = lax.top_k(aff_t, cap)
    ye = _moe_ffn(x1, idx.astype(jnp.int32), gate, w1, w3, w2)
    return _moe_combine(x1, ye, idx.astype(jnp.int32), lg, lb)


def _rope_tables(seq, reps):
    dim = ATT_HEAD_DIM
    inv = 1.0 / (ROPE_THETA ** (jnp.arange(0, dim, 2, dtype=F32) / dim))
    ang = jnp.arange(seq, dtype=F32)[:, None] * inv[None, :]
    ang = jnp.concatenate([ang, ang], -1)
    return jnp.tile(jnp.cos(ang), (1, reps)), jnp.tile(jnp.sin(ang), (1, reps))


def _even_layer(x3, p, s5tabs_fn, lambda_init, l):
    b, s, d = x3.shape
    n = b * s
    x2 = x3.reshape(n, d)
    qk = p['qk']
    av = p['av']
    cos_t, sin_t = _rope_tables(s, 2 * qk // ATT_HEAD_DIM)
    q, k, v, u = _proj_in(x2, p['w_in'], p['w_rot'], cos_t, sin_t, s, qk, av)
    heads = av // ATT_V_DIM
    tk = min(512, s // 2)
    vt = v.reshape(b, s // tk, tk, heads, ATT_V_DIM).transpose(0, 3, 1, 4, 2)
    ot = _diff_attention(q.reshape(b, s, qk), k.reshape(b, s, qk), vt, p['lam'], p['subln_g'], lambda_init, tk)
    attn = ot.transpose(0, 3, 1, 2).reshape(n, av)
    L = min(S5_CHUNK, s)
    nc = s // L
    G, H = p['s5_g'], p['s5_h']
    ug = u.reshape(b, nc, L, G, H).transpose(3, 0, 1, 2, 4).reshape(G, b * nc, L * H)
    yg = _s5_scan(ug, s5tabs_fn(L, nc), nc)
    yg = yg.reshape(G, b, nc, L, H).transpose(1, 2, 3, 0, 4).reshape(n, G * H)
    return _mix_even(x2, attn, yg, p['glu_w'], p['glu_b'], p['w_out_even'], p['ln_mix_g'][l], p['ln_mix_b'][l],
                     p['w_router'][l])


def _odd_layer(x3, p, l):
    b, s, d = x3.shape
    gw = d // FNET_GROUPS
    s2 = 128 if s >= 1024 else s // 8
    s1 = s // s2
    wc, m1, m2, tw_c, tw_s = _dft_tables(s1, s2, gw)
    z = _chan_dft(x3, wc).reshape(b, 2 * s1, s2 * d)
    g = _dft_stage1(m1, z).reshape(b, 2, s1, s2, d)
    f = _dft_stage2(m2, g, tw_c, tw_s, 1.0 / math.sqrt(s * gw))
    return _mix_odd(x3.reshape(b * s, d), f.reshape(b * s, d), p['w_out_odd'], p['ln_mix_g'][l], p['ln_mix_b'][l],
                    p['w_router'][l])


def _trunk(x3, p, s5tabs_fn):
    b, s, d = x3.shape
    for l in range(DEPTH):
        if l % 2 == 0:
            lambda_init = 0.8 - 0.6 * math.exp(-0.3 * l)
            x1, aff = _even_layer(x3, p, s5tabs_fn, lambda_init, l)
        else:
            x1, aff = _odd_layer(x3, p, l)
        x2 = _expert_choice(x1, aff, p['w_ff1'][l], p['w_ff3'][l], p['w_ff2'][l], p['ln_ffn_g'][l], p['ln_ffn_b'][l])
        x3 = x2.reshape(b, s, d)
    return x3


def kernel(x_prompt, x_sample, w_in, lam_q1, lam_k1, lam_q2, lam_k2, subln_g, s5_a_re, s5_a_im, s5_log_dt, s5_b_re, s5_b_im, s5_c_re, s5_c_im, s5_d, s5_glu_w, s5_glu_b, w_out_even, w_out_odd, ln_mix_g, ln_mix_b, w_router, w_ff1, w_ff3, w_ff2, ln_ffn_g, ln_ffn_b):
    assert w_in.shape[0] == 1 and w_out_odd.shape[0] == 1 and DEPTH == 2
    d = w_in.shape[1]
    av = 4 * ATT_V_DIM
    qk = av
    wi = w_in[0]
    wqk = wi[:, :2 * qk].reshape(d, 2 * qk // ATT_HEAD_DIM, 2, ATT_HEAD_DIM // 2)
    w_rot = jnp.stack([-wqk[:, :, 1], wqk[:, :, 0]], axis=2).reshape(d, 2 * qk)
    lam = (jnp.exp(jnp.sum(lam_q1[0] * lam_k1[0])) - jnp.exp(jnp.sum(lam_q2[0] * lam_k2[0]))
           + (0.8 - 0.6 * math.exp(0.0))).reshape(1).astype(F32)
    p = {
        'qk': qk, 'av': av, 's5_g': s5_b_re.shape[2], 's5_h': s5_b_re.shape[4],
        'w_in': wi.astype(BF16), 'w_rot': w_rot.astype(BF16), 'lam': lam,
        'subln_g': subln_g[0].reshape(-1, 1),
        'glu_w': s5_glu_w[0].astype(BF16), 'glu_b': s5_glu_b[0].reshape(1, -1),
        'w_out_even': w_out_even[0].astype(BF16), 'w_out_odd': w_out_odd[0].astype(BF16),
        'ln_mix_g': ln_mix_g[:, None, :], 'ln_mix_b': ln_mix_b[:, None, :],
        'ln_ffn_g': ln_ffn_g[:, None, :], 'ln_ffn_b': ln_ffn_b[:, None, :],
        'w_router': w_router.transpose(0, 2, 1),
        'w_ff1': w_ff1.astype(BF16), 'w_ff3': w_ff3.astype(BF16), 'w_ff2': w_ff2.astype(BF16),
    }

    def s5tabs_fn(chunk, nc):
        nsteps = max(1, (nc - 1).bit_length())
        return _s5_tables(s5_a_re[0], s5_a_im[0], s5_log_dt[0], s5_b_re[0], s5_b_im[0], s5_c_re[0], s5_c_im[0],
                          s5_d[0], chunk, nsteps)

    return (_trunk(x_prompt, p, s5tabs_fn), _trunk(x_sample, p, s5tabs_fn))
```

```python
import functools
import math

import jax
import jax.numpy as jnp
from jax import lax
from jax.experimental import pallas as pl
from jax.experimental.pallas import tpu as pltpu

F32 = jnp.float32
BF16 = jnp.bfloat16
HI = lax.Precision.HIGHEST

ATT_HEAD_DIM = 64
ATT_V_DIM = 2 * ATT_HEAD_DIM
FNET_GROUPS = 4
EC_CAPACITY_FACTOR = 2
ROPE_THETA = 10000.0
LN_EPS = 1e-5
DEPTH = 2
DEEPNORM_ALPHA = (2 * DEPTH) ** 0.25

V7X_VMEM_BYTES = 64 * 1024 * 1024
V7X_LANES = 128

S5_CHUNK = 64


def _cparams(semantics, vmem_mib):
    return pltpu.CompilerParams(dimension_semantics=semantics,
                                vmem_limit_bytes=min(vmem_mib * 1024 * 1024, V7X_VMEM_BYTES * 7 // 8))


def _row_parts(rows, parts=2):
    step = rows // parts if rows % (parts * 2 * V7X_LANES) == 0 else rows
    return [slice(r0, r0 + step) for r0 in range(0, rows, step)]


def _layer_norm(z, g, b):
    mu = jnp.mean(z, -1, keepdims=True)
    zc = z - mu
    var = jnp.mean(zc * zc, -1, keepdims=True)
    return zc * lax.rsqrt(var + LN_EPS) * g + b


def _proj_in_kernel(x_ref, w_ref, wrot_ref, cos_ref, sin_ref, q_ref, k_ref, v_ref, u_ref, *, qk, av, scale):
    for rows in _row_parts(x_ref.shape[0]):
        xb = x_ref[rows, :].astype(BF16)
        h = jnp.dot(xb, w_ref[...], preferred_element_type=F32)
        hr = jnp.dot(xb, wrot_ref[...], preferred_element_type=F32)
        roped = h[:, :2 * qk] * cos_ref[rows, :] + hr * sin_ref[rows, :]
        q_ref[rows, :] = (roped[:, :qk] * scale).astype(BF16)
        k_ref[rows, :] = roped[:, qk:].astype(BF16)
        v_ref[rows, :] = h[:, 2 * qk:2 * qk + av].astype(BF16)
        u_ref[rows, :] = h[:, 2 * qk + av:].astype(BF16)


def _proj_in(x2, w, wrot, cos_t, sin_t, seq, qk, av):
    n, d = x2.shape
    sw = w.shape[1] - 2 * qk - av
    tm = min(512, seq)
    nsb = seq // tm
    kern = functools.partial(_proj_in_kernel, qk=qk, av=av, scale=ATT_HEAD_DIM ** -0.5 * math.log2(math.e))
    return pl.pallas_call(
        kern,
        grid=(n // tm,),
        in_specs=[
            pl.BlockSpec((tm, d), lambda i: (i, 0)),
            pl.BlockSpec(w.shape, lambda i: (0, 0)),
            pl.BlockSpec(wrot.shape, lambda i: (0, 0)),
            pl.BlockSpec((tm, 2 * qk), lambda i: (i % nsb, 0)),
            pl.BlockSpec((tm, 2 * qk), lambda i: (i % nsb, 0)),
        ],
        out_specs=[
            pl.BlockSpec((tm, qk), lambda i: (i, 0)),
            pl.BlockSpec((tm, qk), lambda i: (i, 0)),
            pl.BlockSpec((tm, av), lambda i: (i, 0)),
            pl.BlockSpec((tm, sw), lambda i: (i, 0)),
        ],
        out_shape=[
            jax.ShapeDtypeStruct((n, qk), BF16),
            jax.ShapeDtypeStruct((n, qk), BF16),
            jax.ShapeDtypeStruct((n, av), BF16),
            jax.ShapeDtypeStruct((n, sw), BF16),
        ],
        compiler_params=_cparams(("parallel",), 48),
        name="proj_in",
    )(x2, w, wrot, cos_t, sin_t)


def _attn_kernel(lam_ref, q_ref, k_ref, vt_ref, g_ref, o_ref, acc1, acc2, sa1, sa2, sb1, sb2, *, tk, nkv,
                 out_scale):
    q = q_ref[...]
    tq = q.shape[0]
    lane = lax.broadcasted_iota(jnp.int32, q.shape, 1)
    zero = jnp.zeros_like(q)
    q1 = jnp.where(lane < ATT_HEAD_DIM, q, zero)
    q2 = jnp.where(lane >= ATT_HEAD_DIM, q, zero)
    acc1[...] = jnp.zeros_like(acc1)
    acc2[...] = jnp.zeros_like(acc2)
    nt = (((1,), (1,)), ((), ()))

    def scores(j, d1, d2):
        kb = k_ref[pl.ds(pl.multiple_of(j * tk, tk), tk), :]
        d1[...] = lax.dot_general(kb, q1, nt, preferred_element_type=F32)
        d2[...] = lax.dot_general(kb, q2, nt, preferred_element_type=F32)

    def soft_pv(vb, s_ref, m, l, acc):
        s = s_ref[...]
        mn = jnp.maximum(m, jnp.max(s, axis=0, keepdims=True))
        p = jnp.exp2(s - mn)
        a = jnp.exp2(m - mn)
        l = a * l + jnp.sum(p, axis=0, keepdims=True)
        acc[...] = a * acc[...] + jnp.dot(vb, p.astype(BF16), preferred_element_type=F32)
        return mn, l

    def step(j, cur, nxt, carry, prefetch):
        m1, l1, m2, l2 = carry
        if prefetch:
            scores(j + 1, *nxt)
        vb = vt_ref[j]
        m1, l1 = soft_pv(vb, cur[0], m1, l1, acc1)
        m2, l2 = soft_pv(vb, cur[1], m2, l2, acc2)
        return m1, l1, m2, l2

    buf_a, buf_b = (sa1, sa2), (sb1, sb2)

    def body(i, carry):
        carry = step(2 * i, buf_a, buf_b, carry, True)
        return step(2 * i + 1, buf_b, buf_a, carry, True)

    neg = jnp.full((1, tq), -1e30, F32)
    zer = jnp.zeros((1, tq), F32)
    scores(0, *buf_a)
    carry = lax.fori_loop(0, nkv // 2 - 1, body, (neg, zer, neg, zer))
    carry = step(nkv - 2, buf_a, buf_b, carry, True)
    _, l1, _, l2 = step(nkv - 1, buf_b, buf_a, carry, False)
    o = acc1[...] / l1 - lam_ref[0] * (acc2[...] / l2)
    ms = jnp.mean(o * o, axis=0, keepdims=True)
    o = o * lax.rsqrt(ms + LN_EPS) * g_ref[...] * out_scale
    o_ref[...] = o.astype(o_ref.dtype)


def _diff_attention(q, k, vt, lam, g_col, lambda_init, tk):
    b, s, _ = q.shape
    heads = vt.shape[1]
    nkv = vt.shape[2]
    assert nkv % 2 == 0
    dv = vt.shape[3]
    tq = min(2048, s)
    kern = functools.partial(_attn_kernel, tk=tk, nkv=nkv, out_scale=1.0 - lambda_init)
    return pl.pallas_call(
        kern,
        grid=(b, heads, s // tq),
        in_specs=[
            pl.BlockSpec(memory_space=pltpu.SMEM),
            pl.BlockSpec((None, tq, 2 * ATT_HEAD_DIM), lambda bi, hi, qi: (bi, qi, hi)),
            pl.BlockSpec((None, s, 2 * ATT_HEAD_DIM), lambda bi, hi, qi: (bi, 0, hi)),
            pl.BlockSpec((None, None, nkv, dv, tk), lambda bi, hi, qi: (bi, hi, 0, 0, 0)),
            pl.BlockSpec((dv, 1), lambda bi, hi, qi: (0, 0)),
        ],
        out_specs=pl.BlockSpec((None, None, dv, tq), lambda bi, hi, qi: (bi, hi, 0, qi)),
        out_shape=jax.ShapeDtypeStruct((b, heads, dv, s), BF16),
        scratch_shapes=[pltpu.VMEM((dv, tq), F32)] * 2 + [pltpu.VMEM((tk, tq), F32)] * 4,
        compiler_params=_cparams(("parallel", "parallel", "arbitrary"), 48),
        name="diff_attn",
    )(lam, q, k, vt, g_col)


def _s5_kernel(u_ref, t_ref, we_ref, ws_ref, a1_ref, a2_ref, y_ref, *, nc, nsteps, p2):
    u = u_ref[...]
    r = u.shape[0]
    e = jnp.dot(u, we_ref[...], preferred_element_type=F32)
    c = lax.broadcasted_iota(jnp.int32, (r, p2), 0) % nc
    half = p2 // 2
    hf = e[:, :p2]
    hb = e[:, p2:]
    for kk in range(nsteps):
        sh = 1 << kk
        pf = jnp.where(c >= sh, pltpu.roll(hf, sh, axis=0), 0.0)
        hf = hf + a1_ref[kk:kk + 1, :p2] * pf + a2_ref[kk:kk + 1, :p2] * pltpu.roll(pf, half, axis=1)
        pb = jnp.where(c < nc - sh, pltpu.roll(hb, r - sh, axis=0), 0.0)
        hb = hb + a1_ref[kk:kk + 1, p2:] * pb + a2_ref[kk:kk + 1, p2:] * pltpu.roll(pb, half, axis=1)
    hf_in = jnp.where(c >= 1, pltpu.roll(hf, 1, axis=0), 0.0)
    hb_in = jnp.where(c < nc - 1, pltpu.roll(hb, r - 1, axis=0), 0.0)
    hin = jnp.concatenate([hf_in, hb_in], axis=1).astype(BF16)
    y = jnp.dot(u, t_ref[...], preferred_element_type=F32)
    y = y + jnp.dot(hin, ws_ref[...], preferred_element_type=F32)
    y_ref[...] = jax.nn.gelu(y).astype(BF16)


def _s5_scan(ug, tabs, nc):
    t_tot, we, ws, a1, a2 = tabs
    g, r, lh = ug.shape
    p4 = we.shape[2]
    nsteps = a1.shape[1]
    kern = functools.partial(_s5_kernel, nc=nc, nsteps=nsteps, p2=p4 // 2)
    return pl.pallas_call(
        kern,
        grid=(g,),
        in_specs=[
            pl.BlockSpec((None, r, lh), lambda i: (i, 0, 0)),
            pl.BlockSpec((None, lh, lh), lambda i: (i, 0, 0)),
            pl.BlockSpec((None, lh, p4), lambda i: (i, 0, 0)),
            pl.BlockSpec((None, p4, lh), lambda i: (i, 0, 0)),
            pl.BlockSpec((None, nsteps, p4), lambda i: (i, 0, 0)),
            pl.BlockSpec((None, nsteps, p4), lambda i: (i, 0, 0)),
        ],
        out_specs=pl.BlockSpec((None, r, lh), lambda i: (i, 0, 0)),
        out_shape=jax.ShapeDtypeStruct((g, r, lh), BF16),
        compiler_params=_cparams(("parallel",), 48),
        name="s5_scan",
    )(ug, t_tot, we, ws, a1, a2)


def _s5_tables(a_re, a_im, log_dt, b_re, b_im, c_re, c_im, d, chunk, nsteps):
    L = chunk
    G, P = a_re.shape[1], a_re.shape[2]
    H = b_re.shape[3]
    dt = jnp.exp(log_dt)[..., None]
    mag = jnp.exp(a_re * dt)
    lr = mag * jnp.cos(a_im * dt)
    li = mag * jnp.sin(a_im * dt)
    nr = lr - 1.0
    den = a_re * a_re + a_im * a_im
    cr = (nr * a_re + li * a_im) / den
    ci = (li * a_re - nr * a_im) / den
    bbr = cr[..., None] * b_re - ci[..., None] * b_im
    bbi = cr[..., None] * b_im + ci[..., None] * b_re
    j = jnp.arange(L + 1, dtype=F32)[:, None, None, None]
    pm = jnp.exp(j * (a_re * dt)[None])
    ang = j * (a_im * dt)[None]
    pr = pm * jnp.cos(ang)
    pi = pm * jnp.sin(ang)
    cbr = c_re[..., None] * bbr[:, :, None] - c_im[..., None] * bbi[:, :, None]
    cbi = c_re[..., None] * bbi[:, :, None] + c_im[..., None] * bbr[:, :, None]
    kern = (jnp.einsum('jdgp,dghpk->dgjhk', pr[:L], cbr, precision=HI)
            - jnp.einsum('jdgp,dghpk->dgjhk', pi[:L], cbi, precision=HI))
    s_i = jnp.arange(L)[:, None]
    t_i = jnp.arange(L)[None, :]
    lag_f = jnp.clip(t_i - s_i, 0, L - 1)
    lag_b = jnp.clip(s_i - t_i, 0, L - 1)
    tf = kern[0][:, lag_f] * (t_i >= s_i)[None, :, :, None, None]
    tb = kern[1][:, lag_b] * (s_i >= t_i)[None, :, :, None, None]
    skip = (jnp.eye(L, dtype=F32)[None, :, :, None, None] * jnp.eye(H, dtype=F32)[None, None, None]
            * d[:, None, None, :, None])
    t_tot = (tf + tb + skip).transpose(0, 1, 4, 2, 3).reshape(G, L * H, L * H)
    def state_in(pw_r, pw_i, br, bi):
        re = pw_r[..., None] * br[None] - pw_i[..., None] * bi[None]
        im = pw_r[..., None] * bi[None] + pw_i[..., None] * br[None]
        both = jnp.concatenate([re, im], axis=2)
        return both.transpose(1, 0, 3, 2).reshape(G, L * H, 2 * P)
    we = jnp.concatenate([state_in(pr[:L][::-1, 0], pi[:L][::-1, 0], bbr[0], bbi[0]),
                          state_in(pr[:L, 1], pi[:L, 1], bbr[1], bbi[1])], axis=2)
    def state_out(pw_r, pw_i, cre, cim):
        re = cre[None] * pw_r[:, :, None, :] - cim[None] * pw_i[:, :, None, :]
        im = cre[None] * pw_i[:, :, None, :] + cim[None] * pw_r[:, :, None, :]
        both = jnp.concatenate([re, -im], axis=3)
        return both.transpose(1, 3, 0, 2).reshape(G, 2 * P, L * H)
    ws = jnp.concatenate([state_out(pr[1:, 0], pi[1:, 0], c_re[0], c_im[0]),
                          state_out(pr[1:][::-1, 1], pi[1:][::-1, 1], c_re[1], c_im[1])], axis=1)
    ar, ai = pr[L], pi[L]
    a1, a2 = [], []
    for _ in range(nsteps):
        a1.append(jnp.concatenate([ar[0], ar[0], ar[1], ar[1]], axis=-1))
        a2.append(jnp.concatenate([-ai[0], ai[0], -ai[1], ai[1]], axis=-1))
        ar, ai = ar * ar - ai * ai, 2.0 * ar * ai
    a1 = jnp.stack(a1, axis=1)
    a2 = jnp.stack(a2, axis=1)
    return t_tot.astype(BF16), we.astype(BF16), ws.astype(BF16), a1, a2


def _router_t(x1, wrt_ref):
    logits = lax.dot_general(wrt_ref[...], x1, (((1,), (1,)), ((), ())), preferred_element_type=F32, precision=HI)
    logits = logits - jnp.max(logits, axis=0, keepdims=True)
    ex = jnp.exp(logits)
    return ex / jnp.sum(ex, axis=0, keepdims=True)


def _mix_even_kernel(x_ref, attn_ref, y_ref, gw_ref, gb_ref, wo_ref, lg_ref, lb_ref, wrt_ref,
                     x1_ref, aff_ref, *, aw):
    for rows in _row_parts(x_ref.shape[0]):
        y = y_ref[rows, :]
        gate = jax.nn.sigmoid(jnp.dot(y, gw_ref[...], preferred_element_type=F32) + gb_ref[...])
        ssm = (y.astype(F32) * gate).astype(BF16)
        m = jnp.dot(attn_ref[rows, :], wo_ref[:aw, :], preferred_element_type=F32)
        m = m + jnp.dot(ssm, wo_ref[aw:, :], preferred_element_type=F32)
        x1 = _layer_norm(DEEPNORM_ALPHA * x_ref[rows, :] + m, lg_ref[...], lb_ref[...])
        x1_ref[rows, :] = x1
        aff_ref[:, rows] = _router_t(x1, wrt_ref)


def _mix_odd_kernel(x_ref, f_ref, wo_ref, lg_ref, lb_ref, wrt_ref, x1_ref, aff_ref):
    for rows in _row_parts(x_ref.shape[0]):
        m = jnp.dot(f_ref[rows, :].astype(BF16), wo_ref[...], preferred_element_type=F32)
        x1 = _layer_norm(DEEPNORM_ALPHA * x_ref[rows, :] + m, lg_ref[...], lb_ref[...])
        x1_ref[rows, :] = x1
        aff_ref[:, rows] = _router_t(x1, wrt_ref)


def _full(a):
    nd = a.ndim
    return pl.BlockSpec(a.shape, lambda i: (0,) * nd)


def _rows(tm, width):
    return pl.BlockSpec((tm, width), lambda i: (i, 0))


def _mix_out(n, d, ne, tm):
    specs = [pl.BlockSpec((tm, d), lambda i: (i, 0)), pl.BlockSpec((ne, tm), lambda i: (0, i))]
    shapes = [jax.ShapeDtypeStruct((n, d), F32), jax.ShapeDtypeStruct((ne, n), F32)]
    return specs, shapes


def _mix_even(x2, attn, yg, gw, gb, wo, lg, lb, wrt):
    n, d = x2.shape
    tm = min(512, n)
    aw = attn.shape[1]
    out_specs, out_shape = _mix_out(n, d, wrt.shape[0], tm)
    return pl.pallas_call(
        functools.partial(_mix_even_kernel, aw=aw),
        grid=(n // tm,),
        in_specs=[_rows(tm, d), _rows(tm, aw), _rows(tm, yg.shape[1]), _full(gw), _full(gb), _full(wo),
                  _full(lg), _full(lb), _full(wrt)],
        out_specs=out_specs,
        out_shape=out_shape,
        compiler_params=_cparams(("parallel",), 48),
        name="mix_even",
    )(x2, attn, yg, gw, gb, wo, lg, lb, wrt)


def _mix_odd(x2, f, wo, lg, lb, wrt):
    n, d = x2.shape
    tm = min(512, n)
    out_specs, out_shape = _mix_out(n, d, wrt.shape[0], tm)
    return pl.pallas_call(
        _mix_odd_kernel,
        grid=(n // tm,),
        in_specs=[_rows(tm, d), _rows(tm, d), _full(wo), _full(lg), _full(lb), _full(wrt)],
        out_specs=out_specs,
        out_shape=out_shape,
        compiler_params=_cparams(("parallel",), 48),
        name="mix_odd",
    )(x2, f, wo, lg, lb, wrt)


def _chan_dft_kernel(x_ref, w_ref, z_ref, *, groups, gw):
    x = x_ref[...].astype(BF16)
    w = w_ref[...]
    for gi in range(groups):
        z = jnp.dot(x[:, gi * gw:(gi + 1) * gw], w, preferred_element_type=F32)
        z_ref[0, :, gi * gw:(gi + 1) * gw] = z[:, :gw].astype(BF16)
        z_ref[1, :, gi * gw:(gi + 1) * gw] = z[:, gw:].astype(BF16)


def _chan_dft(x3, w):
    b, s, d = x3.shape
    tm = min(512, s)
    gw = d // FNET_GROUPS
    return pl.pallas_call(
        functools.partial(_chan_dft_kernel, groups=FNET_GROUPS, gw=gw),
        grid=(b, s // tm),
        in_specs=[pl.BlockSpec((None, tm, d), lambda bi, i: (bi, i, 0)),
                  pl.BlockSpec(w.shape, lambda bi, i: (0, 0))],
        out_specs=pl.BlockSpec((None, 2, tm, d), lambda bi, i: (bi, 0, i, 0)),
        out_shape=jax.ShapeDtypeStruct((b, 2, s, d), BF16),
        compiler_params=_cparams(("parallel", "parallel"), 32),
        name="chan_dft",
    )(x3, w)


def _dft_stage1_kernel(m_ref, z_ref, g_ref):
    g_ref[...] = jnp.dot(m_ref[...], z_ref[...], preferred_element_type=F32).astype(BF16)


def _dft_stage1(m1, z):
    b, r, cols = z.shape
    tn = min(8192, cols)
    return pl.pallas_call(
        _dft_stage1_kernel,
        grid=(b, cols // tn),
        in_specs=[pl.BlockSpec(m1.shape, lambda bi, j: (0, 0)),
                  pl.BlockSpec((None, r, tn), lambda bi, j: (bi, 0, j))],
        out_specs=pl.BlockSpec((None, r, tn), lambda bi, j: (bi, 0, j)),
        out_shape=jax.ShapeDtypeStruct((b, r, cols), BF16),
        compiler_params=_cparams(("parallel", "parallel"), 32),
        name="dft_stage1",
    )(m1, z)


def _dft_stage2_kernel(m_ref, g_ref, tc_ref, ts_ref, o_ref, *, kb, d, scale):
    m = m_ref[...]
    for kk in range(kb):
        gr = g_ref[0, kk].astype(F32)
        gi = g_ref[1, kk].astype(F32)
        c = tc_ref[kk]
        s = ts_ref[kk]
        rhs = jnp.concatenate([gr * c + gi * s, gi * c - gr * s], axis=0).astype(BF16)
        o_ref[:, kk * d:(kk + 1) * d] = jnp.dot(m, rhs, preferred_element_type=F32) * scale


def _dft_stage2(m2, g5, tw_c, tw_s, scale):
    b, _, s1, s2, d = g5.shape
    kb = min(8, s1)
    return pl.pallas_call(
        functools.partial(_dft_stage2_kernel, kb=kb, d=d, scale=scale),
        grid=(b, s1 // kb),
        in_specs=[pl.BlockSpec(m2.shape, lambda bi, i: (0, 0)),
                  pl.BlockSpec((None, 2, kb, s2, d), lambda bi, i: (bi, 0, i, 0, 0)),
                  pl.BlockSpec((kb, s2, 1), lambda bi, i: (i, 0, 0)),
                  pl.BlockSpec((kb, s2, 1), lambda bi, i: (i, 0, 0))],
        out_specs=pl.BlockSpec((None, s2, kb * d), lambda bi, i: (bi, 0, i)),
        out_shape=jax.ShapeDtypeStruct((b, s2, s1 * d), F32),
        compiler_params=_cparams(("parallel", "parallel"), 48),
        name="dft_stage2",
    )(m2, g5, tw_c, tw_s)


def _trig(rows, cols, period):
    ang = (2.0 * math.pi / period) * ((rows[:, None] * cols[None, :]) % period).astype(F32)
    return jnp.cos(ang), jnp.sin(ang)


def _dft_tables(s1, s2, gw):
    c = jnp.arange(gw, dtype=jnp.int32)
    cc, sc = _trig(c, c, gw)
    wc = jnp.concatenate([cc, -sc], axis=1).astype(BF16)
    i1 = jnp.arange(s1, dtype=jnp.int32)
    i2 = jnp.arange(s2, dtype=jnp.int32)
    c1, sn1 = _trig(i1, i1, s1)
    m1 = jnp.concatenate([jnp.concatenate([c1, sn1], axis=1),
                          jnp.concatenate([-sn1, c1], axis=1)], axis=0).astype(BF16)
    c2, sn2 = _trig(i2, i2, s2)
    m2 = jnp.concatenate([c2, sn2], axis=1).astype(BF16)
    tw_c, tw_s = _trig(i1, i2, s1 * s2)
    return wc, m1, m2, tw_c[..., None], tw_s[..., None]


def _ffn_kernel(idx_cur, idx_nxt, gate_ref, x_hbm, w1_ref, w3_ref, w2_ref, ye_ref, xbuf, sem, *, tm, ffc):
    s = pl.program_id(0)
    ns = pl.num_programs(0)
    slot = s % 2

    def gather(idx_ref, sl):
        for r in range(tm):
            pltpu.make_async_copy(x_hbm.at[pl.ds(idx_ref[0, 0, r], 1), :], xbuf.at[sl, pl.ds(r, 1), :],
                                  sem.at[sl]).start()

    @pl.when(s == 0)
    def _():
        gather(idx_cur, 0)

    @pl.when((s + 1 < ns) & (slot == 0))
    def _():
        gather(idx_nxt, 1)

    @pl.when((s + 1 < ns) & (slot == 1))
    def _():
        gather(idx_nxt, 0)

    pltpu.make_async_copy(x_hbm.at[pl.ds(0, tm), :], xbuf.at[slot], sem.at[slot]).wait()
    xb = xbuf[slot].astype(BF16)
    dff = w1_ref.shape[1]
    acc = jnp.zeros(ye_ref.shape, F32)
    for c0 in range(0, dff, ffc):
        h1 = jnp.dot(xb, w1_ref[:, c0:c0 + ffc], preferred_element_type=F32)
        h3 = jnp.dot(xb, w3_ref[:, c0:c0 + ffc], preferred_element_type=F32)
        gg = (jax.nn.silu(h1) * h3).astype(BF16)
        acc = acc + jnp.dot(gg, w2_ref[c0:c0 + ffc, :], preferred_element_type=F32)
    ye_ref[...] = acc * gate_ref[...]


def _moe_ffn(x1, idx, gate, w1, w3, w2):
    n, d = x1.shape
    e, cap = idx.shape
    dff = w1.shape[2]
    tm = min(256, cap)
    nblk = cap // tm
    ns = e * nblk
    ffc = dff // 2 if (dff // 2) % V7X_LANES == 0 else dff
    idx3 = idx.reshape(ns, 1, tm)
    gate2 = gate.reshape(e * cap, 1)
    smem_blk = lambda f: pl.BlockSpec((1, 1, tm), f, memory_space=pltpu.SMEM)
    return pl.pallas_call(
        functools.partial(_ffn_kernel, tm=tm, ffc=ffc),
        grid=(ns,),
        in_specs=[
            smem_blk(lambda s: (s, 0, 0)),
            smem_blk(lambda s: (jnp.minimum(s + 1, ns - 1), 0, 0)),
            pl.BlockSpec((tm, 1), lambda s: (s, 0)),
            pl.BlockSpec(memory_space=pl.ANY),
            pl.BlockSpec((None, d, dff), lambda s: (s // nblk, 0, 0)),
            pl.BlockSpec((None, d, dff), lambda s: (s // nblk, 0, 0)),
            pl.BlockSpec((None, dff, d), lambda s: (s // nblk, 0, 0)),
        ],
        out_specs=pl.BlockSpec((tm, d), lambda s: (s, 0)),
        out_shape=jax.ShapeDtypeStruct((e * cap, d), F32),
        scratch_shapes=[pltpu.VMEM((2, tm, d), F32), pltpu.SemaphoreType.DMA((2,))],
        compiler_params=_cparams(("arbitrary",), 56),
        name="moe_ffn",
    )(idx3, idx3, gate2, x1, w1, w3, w2)


def _combine_kernel(tile_ref, chunk_ref, first_ref, last_ref, valid_ref,
                    slot_cur, slot_nxt, tok_ref, x_ref, ye_hbm, lg_ref, lb_ref, o_ref, acc_ref, buf, sem, *, tt, ca):
    w = pl.program_id(0)
    nw = pl.num_programs(0)
    par = w % 2

    def gather(slot_ref, sl):
        for r in range(ca):
            pltpu.make_async_copy(ye_hbm.at[pl.ds(slot_ref[0, 0, r], 1), :], buf.at[sl, pl.ds(r, 1), :],
                                  sem.at[sl]).start(priority=r % 2)

    @pl.when((w == 0) & (valid_ref[0] == 1))
    def _():
        gather(slot_cur, 0)

    nxt_valid = valid_ref[jnp.minimum(w + 1, nw - 1)] * (w + 1 < nw).astype(jnp.int32) == 1

    @pl.when(nxt_valid & (par == 0))
    def _():
        gather(slot_nxt, 1)

    @pl.when(nxt_valid & (par == 1))
    def _():
        gather(slot_nxt, 0)

    @pl.when(first_ref[w] == 1)
    def _():
        acc_ref[...] = jnp.zeros_like(acc_ref)

    @pl.when(valid_ref[w] == 1)
    def _():
        pltpu.make_async_copy(ye_hbm.at[pl.ds(0, ca), :], buf.at[par], sem.at[par]).wait()
        t0 = tile_ref[w] * tt
        row = lax.broadcasted_iota(jnp.int32, (tt, ca), 0) + t0
        onehot = jnp.where(tok_ref[0] == row, 1.0, 0.0).astype(BF16)
        acc_ref[...] += jnp.dot(onehot, buf[par].astype(BF16), preferred_element_type=F32)

    @pl.when(last_ref[w] == 1)
    def _():
        o_ref[...] = _layer_norm(DEEPNORM_ALPHA * x_ref[...] + acc_ref[...], lg_ref[...], lb_ref[...])


def _moe_combine(x1, ye, idx, lg, lb):
    n, d = x1.shape
    na = ye.shape[0]
    tt = min(256, n)
    ca = min(256, na)
    nt = n // tt
    nch = na // ca
    nw = nt + nch
    tok_flat = idx.reshape(-1)
    tok_sorted, order = lax.sort((tok_flat, jnp.arange(na, dtype=jnp.int32)), dimension=0, num_keys=1,
                                 is_stable=False)
    bounds = jnp.searchsorted(tok_sorted, jnp.arange(nt + 1, dtype=jnp.int32) * tt, side='left').astype(jnp.int32)
    start, end = bounds[:-1], bounds[1:]
    c_lo = jnp.minimum(start // ca, nch - 1)
    c_hi = jnp.where(end > start, (end - 1) // ca, c_lo)
    cnt = c_hi - c_lo + 1
    off = jnp.cumsum(cnt) - cnt
    total = off[-1] + cnt[-1]
    wi = jnp.arange(nw, dtype=jnp.int32)
    tile_w = jnp.clip(jnp.searchsorted(off, wi, side='right').astype(jnp.int32) - 1, 0, nt - 1)
    valid = wi < total
    rel = wi - off[tile_w]
    chunk_w = jnp.where(valid, c_lo[tile_w] + rel, nch - 1).astype(jnp.int32)
    first = (valid & (rel == 0)).astype(jnp.int32)
    last = (valid & (rel == cnt[tile_w] - 1)).astype(jnp.int32)
    tile_w = jnp.where(valid, tile_w, nt - 1).astype(jnp.int32)
    valid = valid.astype(jnp.int32)
    slot3 = order.reshape(nch, 1, ca)
    tok3 = tok_sorted.reshape(nch, 1, ca)
    grid_spec = pltpu.PrefetchScalarGridSpec(
        num_scalar_prefetch=5,
        grid=(nw,),
        in_specs=[
            pl.BlockSpec((1, 1, ca), lambda w, tl, ch, fi, la, va: (ch[w], 0, 0), memory_space=pltpu.SMEM),
            pl.BlockSpec((1, 1, ca), lambda w, tl, ch, fi, la, va: (ch[jnp.minimum(w + 1, nw - 1)], 0, 0),
                         memory_space=pltpu.SMEM),
            pl.BlockSpec((1, 1, ca), lambda w, tl, ch, fi, la, va: (ch[w], 0, 0)),
            pl.BlockSpec((tt, d), lambda w, tl, ch, fi, la, va: (tl[w], 0)),
            pl.BlockSpec(memory_space=pl.ANY),
            pl.BlockSpec((1, d), lambda w, tl, ch, fi, la, va: (0, 0)),
            pl.BlockSpec((1, d), lambda w, tl, ch, fi, la, va: (0, 0)),
        ],
        out_specs=pl.BlockSpec((tt, d), lambda w, tl, ch, fi, la, va: (tl[w], 0)),
        scratch_shapes=[pltpu.VMEM((tt, d), F32), pltpu.VMEM((2, ca, d), F32), pltpu.SemaphoreType.DMA((2,))],
    )
    return pl.pallas_call(
        functools.partial(_combine_kernel, tt=tt, ca=ca),
        grid_spec=grid_spec,
        out_shape=jax.ShapeDtypeStruct((n, d), F32),
        compiler_params=_cparams(("arbitrary",), 32),
        name="moe_combine",
    )(tile_w, chunk_w, first, last, valid, slot3, slot3, tok3, x1, ye, lg, lb)


def _expert_choice(x1, aff_t, w1, w3, w2, lg, lb):
    ne, n = aff_t.shape
    cap = max(1, EC_CAPACITY_FACTOR * n // ne)
    gate, idx = lax.top_k(aff_t, cap)
    ye = _moe_ffn(x1, idx.astype(jnp.int32), gate, w1, w3, w2)
    return _moe_combine(x1, ye, idx.astype(jnp.int32), lg, lb)


def _rope_tables(seq, reps):
    dim = ATT_HEAD_DIM
    inv = 1.0 / (ROPE_THETA ** (jnp.arange(0, dim, 2, dtype=F32) / dim))
    ang = jnp.arange(seq, dtype=F32)[:, None] * inv[None, :]
    ang = jnp.concatenate([ang, ang], -1)
    return jnp.tile(jnp.cos(ang), (1, reps)), jnp.tile(jnp.sin(ang), (1, reps))


def _even_layer(x3, p, s5tabs_fn, lambda_init, l):
    b, s, d = x3.shape
    n = b * s
    x2 = x3.reshape(n, d)
    qk = p['qk']
    av = p['av']
    cos_t, sin_t = _rope_tables(s, 2 * qk // ATT_HEAD_DIM)
    q, k, v, u = _proj_in(x2, p['w_in'], p['w_rot'], cos_t, sin_t, s, qk, av)
    heads = av // ATT_V_DIM
    tk = min(512, s // 2)
    vt = v.reshape(b, s // tk, tk, heads, ATT_V_DIM).transpose(0, 3, 1, 4, 2)
    ot = _diff_attention(q.reshape(b, s, qk), k.reshape(b, s, qk), vt, p['lam'], p['subln_g'], lambda_init, tk)
    attn = ot.transpose(0, 3, 1, 2).reshape(n, av)
    L = min(S5_CHUNK, s)
    nc = s // L
    G, H = p['s5_g'], p['s5_h']
    ug = u.reshape(b, nc, L, G, H).transpose(3, 0, 1, 2, 4).reshape(G, b * nc, L * H)
    yg = _s5_scan(ug, s5tabs_fn(L, nc), nc)
    yg = yg.reshape(G, b, nc, L, H).transpose(1, 2, 3, 0, 4).reshape(n, G * H)
    return _mix_even(x2, attn, yg, p['glu_w'], p['glu_b'], p['w_out_even'], p['ln_mix_g'][l], p['ln_mix_b'][l],
                     p['w_router'][l])


def _odd_layer(x3, p, l):
    b, s, d = x3.shape
    gw = d // FNET_GROUPS
    s2 = 128 if s >= 1024 else s // 8
    s1 = s // s2
    wc, m1, m2, tw_c, tw_s = _dft_tables(s1, s2, gw)
    z = _chan_dft(x3, wc).reshape(b, 2 * s1, s2 * d)
    g = _dft_stage1(m1, z).reshape(b, 2, s1, s2, d)
    f = _dft_stage2(m2, g, tw_c, tw_s, 1.0 / math.sqrt(s * gw))
    return _mix_odd(x3.reshape(b * s, d), f.reshape(b * s, d), p['w_out_odd'], p['ln_mix_g'][l], p['ln_mix_b'][l],
                    p['w_router'][l])


def _trunk(x3, p, s5tabs_fn):
    b, s, d = x3.shape
    for l in range(DEPTH):
        if l % 2 == 0:
            lambda_init = 0.8 - 0.6 * math.exp(-0.3 * l)
            x1, aff = _even_layer(x3, p, s5tabs_fn, lambda_init, l)
        else:
            x1, aff = _odd_layer(x3, p, l)
        x2 = _expert_choice(x1, aff, p['w_ff1'][l], p['w_ff3'][l], p['w_ff2'][l], p['ln_ffn_g'][l], p['ln_ffn_b'][l])
        x3 = x2.reshape(b, s, d)
    return x3


def kernel(x_prompt, x_sample, w_in, lam_q1, lam_k1, lam_q2, lam_k2, subln_g, s5_a_re, s5_a_im, s5_log_dt, s5_b_re, s5_b_im, s5_c_re, s5_c_im, s5_d, s5_glu_w, s5_glu_b, w_out_even, w_out_odd, ln_mix_g, ln_mix_b, w_router, w_ff1, w_ff3, w_ff2, ln_ffn_g, ln_ffn_b):
    assert w_in.shape[0] == 1 and w_out_odd.shape[0] == 1 and DEPTH == 2
    d = w_in.shape[1]
    av = 4 * ATT_V_DIM
    qk = av
    wi = w_in[0]
    wqk = wi[:, :2 * qk].reshape(d, 2 * qk // ATT_HEAD_DIM, 2, ATT_HEAD_DIM // 2)
    w_rot = jnp.stack([-wqk[:, :, 1], wqk[:, :, 0]], axis=2).reshape(d, 2 * qk)
    lam = (jnp.exp(jnp.sum(lam_q1[0] * lam_k1[0])) - jnp.exp(jnp.sum(lam_q2[0] * lam_k2[0]))
           + (0.8 - 0.6 * math.exp(0.0))).reshape(1).astype(F32)
    p = {
        'qk': qk, 'av': av, 's5_g': s5_b_re.shape[2], 's5_h': s5_b_re.shape[4],
        'w_in': wi.astype(BF16), 'w_rot': w_rot.astype(BF16), 'lam': lam,
        'subln_g': subln_g[0].reshape(-1, 1),
        'glu_w': s5_glu_w[0].astype(BF16), 'glu_b': s5_glu_b[0].reshape(1, -1),
        'w_out_even': w_out_even[0].astype(BF16), 'w_out_odd': w_out_odd[0].astype(BF16),
        'ln_mix_g': ln_mix_g[:, None, :], 'ln_mix_b': ln_mix_b[:, None, :],
        'ln_ffn_g': ln_ffn_g[:, None, :], 'ln_ffn_b': ln_ffn_b[:, None, :],
        'w_router': w_router.transpose(0, 2, 1),
        'w_ff1': w_ff1.astype(BF16), 'w_ff3': w_ff3.astype(BF16), 'w_ff2': w_ff2.astype(BF16),
    }

    def s5tabs_fn(chunk, nc):
        nsteps = max(1, (nc - 1).bit_length())
        return _s5_tables(s5_a_re[0], s5_a_im[0], s5_log_dt[0], s5_b_re[0], s5_b_im[0], s5_c_re[0], s5_c_im[0],
                          s5_d[0], chunk, nsteps)

    return (_trunk(x_prompt, p, s5tabs_fn), _trunk(x_sample, p, s5tabs_fn))
```

```python
import functools
import math

import jax
import jax.numpy as jnp
from jax import lax
from jax.experimental import pallas as pl
from jax.experimental.pallas import tpu as pltpu

F32 = jnp.float32
BF16 = jnp.bfloat16
HI = lax.Precision.HIGHEST

ATT_HEAD_DIM = 64
ATT_V_DIM = 2 * ATT_HEAD_DIM
FNET_GROUPS = 4
EC_CAPACITY_FACTOR = 2
ROPE_THETA = 10000.0
LN_EPS = 1e-5
DEPTH = 2
DEEPNORM_ALPHA = (2 * DEPTH) ** 0.25

V7X_VMEM_BYTES = 64 * 1024 * 1024
V7X_LANES = 128

S5_CHUNK = 64


def _cparams(semantics, vmem_mib):
    return pltpu.CompilerParams(dimension_semantics=semantics,
                                vmem_limit_bytes=min(vmem_mib * 1024 * 1024, V7X_VMEM_BYTES * 7 // 8))


def _row_parts(rows, parts=2):
    step = rows // parts if rows % (parts * 2 * V7X_LANES) == 0 else rows
    return [slice(r0, r0 + step) for r0 in range(0, rows, step)]


def _layer_norm(z, g, b):
    mu = jnp.mean(z, -1, keepdims=True)
    zc = z - mu
    var = jnp.mean(zc * zc, -1, keepdims=True)
    return zc * lax.rsqrt(var + LN_EPS) * g + b


def _proj_in_kernel(x_ref, w_ref, wrot_ref, cos_ref, sin_ref, q_ref, k_ref, v_ref, u_ref, *, qk, av, scale):
    for rows in _row_parts(x_ref.shape[0]):
        xb = x_ref[rows, :].astype(BF16)
        h = jnp.dot(xb, w_ref[...], preferred_element_type=F32)
        hr = jnp.dot(xb, wrot_ref[...], preferred_element_type=F32)
        roped = h[:, :2 * qk] * cos_ref[rows, :] + hr * sin_ref[rows, :]
        q_ref[rows, :] = (roped[:, :qk] * scale).astype(BF16)
        k_ref[rows, :] = roped[:, qk:].astype(BF16)
        v_ref[rows, :] = h[:, 2 * qk:2 * qk + av].astype(BF16)
        u_ref[rows, :] = h[:, 2 * qk + av:].astype(BF16)


def _proj_in(x2, w, wrot, cos_t, sin_t, seq, qk, av):
    n, d = x2.shape
    sw = w.shape[1] - 2 * qk - av
    tm = min(512, seq)
    nsb = seq // tm
    kern = functools.partial(_proj_in_kernel, qk=qk, av=av, scale=ATT_HEAD_DIM ** -0.5 * math.log2(math.e))
    return pl.pallas_call(
        kern,
        grid=(n // tm,),
        in_specs=[
            pl.BlockSpec((tm, d), lambda i: (i, 0)),
            pl.BlockSpec(w.shape, lambda i: (0, 0)),
            pl.BlockSpec(wrot.shape, lambda i: (0, 0)),
            pl.BlockSpec((tm, 2 * qk), lambda i: (i % nsb, 0)),
            pl.BlockSpec((tm, 2 * qk), lambda i: (i % nsb, 0)),
        ],
        out_specs=[
            pl.BlockSpec((tm, qk), lambda i: (i, 0)),
            pl.BlockSpec((tm, qk), lambda i: (i, 0)),
            pl.BlockSpec((tm, av), lambda i: (i, 0)),
            pl.BlockSpec((tm, sw), lambda i: (i, 0)),
        ],
        out_shape=[
            jax.ShapeDtypeStruct((n, qk), BF16),
            jax.ShapeDtypeStruct((n, qk), BF16),
            jax.ShapeDtypeStruct((n, av), BF16),
            jax.ShapeDtypeStruct((n, sw), BF16),
        ],
        compiler_params=_cparams(("parallel",), 48),
        name="proj_in",
    )(x2, w, wrot, cos_t, sin_t)


def _attn_kernel(lam_ref, q_ref, k_ref, vt_ref, g_ref, o_ref, acc1, acc2, sa1, sa2, sb1, sb2, *, tk, nkv,
                 out_scale):
    q = q_ref[...]
    tq = q.shape[0]
    lane = lax.broadcasted_iota(jnp.int32, q.shape, 1)
    zero = jnp.zeros_like(q)
    q1 = jnp.where(lane < ATT_HEAD_DIM, q, zero)
    q2 = jnp.where(lane >= ATT_HEAD_DIM, q, zero)
    acc1[...] = jnp.zeros_like(acc1)
    acc2[...] = jnp.zeros_like(acc2)
    nt = (((1,), (1,)), ((), ()))

    def scores(j, d1, d2):
        kb = k_ref[pl.ds(pl.multiple_of(j * tk, tk), tk), :]
        d1[...] = lax.dot_general(kb, q1, nt, preferred_element_type=F32)
        d2[...] = lax.dot_general(kb, q2, nt, preferred_element_type=F32)

    def soft_pv(vb, s_ref, m, l, acc):
        s = s_ref[...]
        mn = jnp.maximum(m, jnp.max(s, axis=0, keepdims=True))
        p = jnp.exp2(s - mn)
        a = jnp.exp2(m - mn)
        l = a * l + jnp.sum(p, axis=0, keepdims=True)
        acc[...] = a * acc[...] + jnp.dot(vb, p.astype(BF16), preferred_element_type=F32)
        return mn, l

    def step(j, cur, nxt, carry, prefetch):
        m1, l1, m2, l2 = carry
        if prefetch:
            scores(j + 1, *nxt)
        vb = vt_ref[j]
        m1, l1 = soft_pv(vb, cur[0], m1, l1, acc1)
        m2, l2 = soft_pv(vb, cur[1], m2, l2, acc2)
        return m1, l1, m2, l2

    buf_a, buf_b = (sa1, sa2), (sb1, sb2)

    def body(i, carry):
        carry = step(2 * i, buf_a, buf_b, carry, True)
        return step(2 * i + 1, buf_b, buf_a, carry, True)

    neg = jnp.full((1, tq), -1e30, F32)
    zer = jnp.zeros((1, tq), F32)
    scores(0, *buf_a)
    carry = lax.fori_loop(0, nkv // 2 - 1, body, (neg, zer, neg, zer))
    carry = step(nkv - 2, buf_a, buf_b, carry, True)
    _, l1, _, l2 = step(nkv - 1, buf_b, buf_a, carry, False)
    o = acc1[...] / l1 - lam_ref[0] * (acc2[...] / l2)
    ms = jnp.mean(o * o, axis=0, keepdims=True)
    o = o * lax.rsqrt(ms + LN_EPS) * g_ref[...] * out_scale
    o_ref[...] = o.astype(o_ref.dtype)


def _diff_attention(q, k, vt, lam, g_col, lambda_init, tk):
    b, s, _ = q.shape
    heads = vt.shape[1]
    nkv = vt.shape[2]
    assert nkv % 2 == 0
    dv = vt.shape[3]
    tq = min(2048, s)
    kern = functools.partial(_attn_kernel, tk=tk, nkv=nkv, out_scale=1.0 - lambda_init)
    return pl.pallas_call(
        kern,
        grid=(b, heads, s // tq),
        in_specs=[
            pl.BlockSpec(memory_space=pltpu.SMEM),
            pl.BlockSpec((None, tq, 2 * ATT_HEAD_DIM), lambda bi, hi, qi: (bi, qi, hi)),
            pl.BlockSpec((None, s, 2 * ATT_HEAD_DIM), lambda bi, hi, qi: (bi, 0, hi)),
            pl.BlockSpec((None, None, nkv, dv, tk), lambda bi, hi, qi: (bi, hi, 0, 0, 0)),
            pl.BlockSpec((dv, 1), lambda bi, hi, qi: (0, 0)),
        ],
        out_specs=pl.BlockSpec((None, None, dv, tq), lambda bi, hi, qi: (bi, hi, 0, qi)),
        out_shape=jax.ShapeDtypeStruct((b, heads, dv, s), BF16),
        scratch_shapes=[pltpu.VMEM((dv, tq), F32)] * 2 + [pltpu.VMEM((tk, tq), F32)] * 4,
        compiler_params=_cparams(("parallel", "parallel", "arbitrary"), 48),
        name="diff_attn",
    )(lam, q, k, vt, g_col)


def _s5_kernel(u_ref, t_ref, we_ref, ws_ref, a1_ref, a2_ref, y_ref, *, nc, nsteps, p2):
    u = u_ref[...]
    r = u.shape[0]
    e = jnp.dot(u, we_ref[...], preferred_element_type=F32)
    c = lax.broadcasted_iota(jnp.int32, (r, p2), 0) % nc
    half = p2 // 2
    hf = e[:, :p2]
    hb = e[:, p2:]
    for kk in range(nsteps):
        sh = 1 << kk
        pf = jnp.where(c >= sh, pltpu.roll(hf, sh, axis=0), 0.0)
        hf = hf + a1_ref[kk:kk + 1, :p2] * pf + a2_ref[kk:kk + 1, :p2] * pltpu.roll(pf, half, axis=1)
        pb = jnp.where(c < nc - sh, pltpu.roll(hb, r - sh, axis=0), 0.0)
        hb = hb + a1_ref[kk:kk + 1, p2:] * pb + a2_ref[kk:kk + 1, p2:] * pltpu.roll(pb, half, axis=1)
    hf_in = jnp.where(c >= 1, pltpu.roll(hf, 1, axis=0), 0.0)
    hb_in = jnp.where(c < nc - 1, pltpu.roll(hb, r - 1, axis=0), 0.0)
    hin = jnp.concatenate([hf_in, hb_in], axis=1).astype(BF16)
    y = jnp.dot(u, t_ref[...], preferred_element_type=F32)
    y = y + jnp.dot(hin, ws_ref[...], preferred_element_type=F32)
    y_ref[...] = jax.nn.gelu(y).astype(BF16)


def _s5_scan(ug, tabs, nc):
    t_tot, we, ws, a1, a2 = tabs
    g, r, lh = ug.shape
    p4 = we.shape[2]
    nsteps = a1.shape[1]
    kern = functools.partial(_s5_kernel, nc=nc, nsteps=nsteps, p2=p4 // 2)
    return pl.pallas_call(
        kern,
        grid=(g,),
        in_specs=[
            pl.BlockSpec((None, r, lh), lambda i: (i, 0, 0)),
            pl.BlockSpec((None, lh, lh), lambda i: (i, 0, 0)),
            pl.BlockSpec((None, lh, p4), lambda i: (i, 0, 0)),
            pl.BlockSpec((None, p4, lh), lambda i: (i, 0, 0)),
            pl.BlockSpec((None, nsteps, p4), lambda i: (i, 0, 0)),
            pl.BlockSpec((None, nsteps, p4), lambda i: (i, 0, 0)),
        ],
        out_specs=pl.BlockSpec((None, r, lh), lambda i: (i, 0, 0)),
        out_shape=jax.ShapeDtypeStruct((g, r, lh), BF16),
        compiler_params=_cparams(("parallel",), 48),
        name="s5_scan",
    )(ug, t_tot, we, ws, a1, a2)


def _s5_tables(a_re, a_im, log_dt, b_re, b_im, c_re, c_im, d, chunk, nsteps):
    L = chunk
    G, P = a_re.shape[1], a_re.shape[2]
    H = b_re.shape[3]
    dt = jnp.exp(log_dt)[..., None]
    mag = jnp.exp(a_re * dt)
    lr = mag * jnp.cos(a_im * dt)
    li = mag * jnp.sin(a_im * dt)
    nr = lr - 1.0
    den = a_re * a_re + a_im * a_im
    cr = (nr * a_re + li * a_im) / den
    ci = (li * a_re - nr * a_im) / den
    bbr = cr[..., None] * b_re - ci[..., None] * b_im
    bbi = cr[..., None] * b_im + ci[..., None] * b_re
    j = jnp.arange(L + 1, dtype=F32)[:, None, None, None]
    pm = jnp.exp(j * (a_re * dt)[None])
    ang = j * (a_im * dt)[None]
    pr = pm * jnp.cos(ang)
    pi = pm * jnp.sin(ang)
    cbr = c_re[..., None] * bbr[:, :, None] - c_im[..., None] * bbi[:, :, None]
    cbi = c_re[..., None] * bbi[:, :, None] + c_im[..., None] * bbr[:, :, None]
    kern = (jnp.einsum('jdgp,dghpk->dgjhk', pr[:L], cbr, precision=HI)
            - jnp.einsum('jdgp,dghpk->dgjhk', pi[:L], cbi, precision=HI))
    lag0 = kern[0][:, :1] + kern[1][:, :1] + jnp.eye(H, dtype=F32)[None, None] * d[:, None, :, None]
    kc = jnp.concatenate([kern[1][:, :0:-1], lag0, kern[0][:, 1:]], axis=1)
    kc = kc.transpose(0, 1, 3, 2).astype(BF16)
    lag = jnp.arange(L)[None, :] - jnp.arange(L)[:, None] + (L - 1)
    t_tot = kc[:, lag].transpose(0, 1, 3, 2, 4).reshape(G, L * H, L * H)
    def state_in(pw_r, pw_i, br, bi):
        re = pw_r[..., None] * br[None] - pw_i[..., None] * bi[None]
        im = pw_r[..., None] * bi[None] + pw_i[..., None] * br[None]
        both = jnp.concatenate([re, im], axis=2)
        return both.transpose(1, 0, 3, 2).reshape(G, L * H, 2 * P)
    we = jnp.concatenate([state_in(pr[:L][::-1, 0], pi[:L][::-1, 0], bbr[0], bbi[0]),
                          state_in(pr[:L, 1], pi[:L, 1], bbr[1], bbi[1])], axis=2)
    def state_out(pw_r, pw_i, cre, cim):
        re = cre[None] * pw_r[:, :, None, :] - cim[None] * pw_i[:, :, None, :]
        im = cre[None] * pw_i[:, :, None, :] + cim[None] * pw_r[:, :, None, :]
        both = jnp.concatenate([re, -im], axis=3)
        return both.transpose(1, 3, 0, 2).reshape(G, 2 * P, L * H)
    ws = jnp.concatenate([state_out(pr[1:, 0], pi[1:, 0], c_re[0], c_im[0]),
                          state_out(pr[1:][::-1, 1], pi[1:][::-1, 1], c_re[1], c_im[1])], axis=1)
    ar, ai = pr[L], pi[L]
    a1, a2 = [], []
    for _ in range(nsteps):
        a1.append(jnp.concatenate([ar[0], ar[0], ar[1], ar[1]], axis=-1))
        a2.append(jnp.concatenate([-ai[0], ai[0], -ai[1], ai[1]], axis=-1))
        ar, ai = ar * ar - ai * ai, 2.0 * ar * ai
    a1 = jnp.stack(a1, axis=1)
    a2 = jnp.stack(a2, axis=1)
    return t_tot, we.astype(BF16), ws.astype(BF16), a1, a2


def _router_t(x1, wrt_ref):
    logits = lax.dot_general(wrt_ref[...], x1, (((1,), (1,)), ((), ())), preferred_element_type=F32, precision=HI)
    logits = logits - jnp.max(logits, axis=0, keepdims=True)
    ex = jnp.exp(logits)
    return ex / jnp.sum(ex, axis=0, keepdims=True)


def _mix_even_kernel(x_ref, attn_ref, y_ref, gw_ref, gb_ref, wo_ref, lg_ref, lb_ref, wrt_ref,
                     x1_ref, aff_ref, *, aw):
    for rows in _row_parts(x_ref.shape[0]):
        y = y_ref[rows, :]
        gate = jax.nn.sigmoid(jnp.dot(y, gw_ref[...], preferred_element_type=F32) + gb_ref[...])
        ssm = (y.astype(F32) * gate).astype(BF16)
        m = jnp.dot(attn_ref[rows, :], wo_ref[:aw, :], preferred_element_type=F32)
        m = m + jnp.dot(ssm, wo_ref[aw:, :], preferred_element_type=F32)
        x1 = _layer_norm(DEEPNORM_ALPHA * x_ref[rows, :] + m, lg_ref[...], lb_ref[...])
        x1_ref[rows, :] = x1
        aff_ref[:, rows] = _router_t(x1, wrt_ref)


def _mix_odd_kernel(x_ref, f_ref, wo_ref, lg_ref, lb_ref, wrt_ref, x1_ref, aff_ref):
    for rows in _row_parts(x_ref.shape[0]):
        m = jnp.dot(f_ref[rows, :].astype(BF16), wo_ref[...], preferred_element_type=F32)
        x1 = _layer_norm(DEEPNORM_ALPHA * x_ref[rows, :] + m, lg_ref[...], lb_ref[...])
        x1_ref[rows, :] = x1
        aff_ref[:, rows] = _router_t(x1, wrt_ref)


def _full(a):
    nd = a.ndim
    return pl.BlockSpec(a.shape, lambda i: (0,) * nd)


def _rows(tm, width):
    return pl.BlockSpec((tm, width), lambda i: (i, 0))


def _mix_out(n, d, ne, tm):
    specs = [pl.BlockSpec((tm, d), lambda i: (i, 0)), pl.BlockSpec((ne, tm), lambda i: (0, i))]
    shapes = [jax.ShapeDtypeStruct((n, d), F32), jax.ShapeDtypeStruct((ne, n), F32)]
    return specs, shapes


def _mix_even(x2, attn, yg, gw, gb, wo, lg, lb, wrt):
    n, d = x2.shape
    tm = min(512, n)
    aw = attn.shape[1]
    out_specs, out_shape = _mix_out(n, d, wrt.shape[0], tm)
    return pl.pallas_call(
        functools.partial(_mix_even_kernel, aw=aw),
        grid=(n // tm,),
        in_specs=[_rows(tm, d), _rows(tm, aw), _rows(tm, yg.shape[1]), _full(gw), _full(gb), _full(wo),
                  _full(lg), _full(lb), _full(wrt)],
        out_specs=out_specs,
        out_shape=out_shape,
        compiler_params=_cparams(("parallel",), 48),
        name="mix_even",
    )(x2, attn, yg, gw, gb, wo, lg, lb, wrt)


def _mix_odd(x2, f, wo, lg, lb, wrt):
    n, d = x2.shape
    tm = min(512, n)
    out_specs, out_shape = _mix_out(n, d, wrt.shape[0], tm)
    return pl.pallas_call(
        _mix_odd_kernel,
        grid=(n // tm,),
        in_specs=[_rows(tm, d), _rows(tm, d), _full(wo), _full(lg), _full(lb), _full(wrt)],
        out_specs=out_specs,
        out_shape=out_shape,
        compiler_params=_cparams(("parallel",), 48),
        name="mix_odd",
    )(x2, f, wo, lg, lb, wrt)


def _chan_dft_kernel(x_ref, w_ref, z_ref, *, groups, gw):
    x = x_ref[...].astype(BF16)
    w = w_ref[...]
    for gi in range(groups):
        z = jnp.dot(x[:, gi * gw:(gi + 1) * gw], w, preferred_element_type=F32)
        z_ref[0, :, gi * gw:(gi + 1) * gw] = z[:, :gw].astype(BF16)
        z_ref[1, :, gi * gw:(gi + 1) * gw] = z[:, gw:].astype(BF16)


def _chan_dft(x3, w):
    b, s, d = x3.shape
    tm = min(512, s)
    gw = d // FNET_GROUPS
    return pl.pallas_call(
        functools.partial(_chan_dft_kernel, groups=FNET_GROUPS, gw=gw),
        grid=(b, s // tm),
        in_specs=[pl.BlockSpec((None, tm, d), lambda bi, i: (bi, i, 0)),
                  pl.BlockSpec(w.shape, lambda bi, i: (0, 0))],
        out_specs=pl.BlockSpec((None, 2, tm, d), lambda bi, i: (bi, 0, i, 0)),
        out_shape=jax.ShapeDtypeStruct((b, 2, s, d), BF16),
        compiler_params=_cparams(("parallel", "parallel"), 32),
        name="chan_dft",
    )(x3, w)


def _dft_stage1_kernel(m_ref, z_ref, g_ref):
    g_ref[...] = jnp.dot(m_ref[...], z_ref[...], preferred_element_type=F32).astype(BF16)


def _dft_stage1(m1, z):
    b, r, cols = z.shape
    tn = min(8192, cols)
    return pl.pallas_call(
        _dft_stage1_kernel,
        grid=(b, cols // tn),
        in_specs=[pl.BlockSpec(m1.shape, lambda bi, j: (0, 0)),
                  pl.BlockSpec((None, r, tn), lambda bi, j: (bi, 0, j))],
        out_specs=pl.BlockSpec((None, r, tn), lambda bi, j: (bi, 0, j)),
        out_shape=jax.ShapeDtypeStruct((b, r, cols), BF16),
        compiler_params=_cparams(("parallel", "parallel"), 32),
        name="dft_stage1",
    )(m1, z)


def _dft_stage2_kernel(m_ref, g_ref, tc_ref, ts_ref, o_ref, *, kb, d, scale):
    m = m_ref[...]
    for kk in range(kb):
        gr = g_ref[0, kk].astype(F32)
        gi = g_ref[1, kk].astype(F32)
        c = tc_ref[kk]
        s = ts_ref[kk]
        rhs = jnp.concatenate([gr * c + gi * s, gi * c - gr * s], axis=0).astype(BF16)
        o_ref[:, kk, :] = jnp.dot(m, rhs, preferred_element_type=F32) * scale


def _dft_stage2(m2, g5, tw_c, tw_s, scale):
    b, _, s1, s2, d = g5.shape
    kb = min(8, s1)
    return pl.pallas_call(
        functools.partial(_dft_stage2_kernel, kb=kb, d=d, scale=scale),
        grid=(b, s1 // kb),
        in_specs=[pl.BlockSpec(m2.shape, lambda bi, i: (0, 0)),
                  pl.BlockSpec((None, 2, kb, s2, d), lambda bi, i: (bi, 0, i, 0, 0)),
                  pl.BlockSpec((kb, s2, 1), lambda bi, i: (i, 0, 0)),
                  pl.BlockSpec((kb, s2, 1), lambda bi, i: (i, 0, 0))],
        out_specs=pl.BlockSpec((None, s2, kb, d), lambda bi, i: (bi, 0, i, 0)),
        out_shape=jax.ShapeDtypeStruct((b, s2, s1, d), F32),
        compiler_params=_cparams(("parallel", "parallel"), 48),
        name="dft_stage2",
    )(m2, g5, tw_c, tw_s)


def _trig(rows, cols, period):
    ang = (2.0 * math.pi / period) * ((rows[:, None] * cols[None, :]) % period).astype(F32)
    return jnp.cos(ang), jnp.sin(ang)


def _dft_tables(s1, s2, gw):
    c = jnp.arange(gw, dtype=jnp.int32)
    cc, sc = _trig(c, c, gw)
    wc = jnp.concatenate([cc, -sc], axis=1).astype(BF16)
    i1 = jnp.arange(s1, dtype=jnp.int32)
    i2 = jnp.arange(s2, dtype=jnp.int32)
    c1, sn1 = _trig(i1, i1, s1)
    m1 = jnp.concatenate([jnp.concatenate([c1, sn1], axis=1),
                          jnp.concatenate([-sn1, c1], axis=1)], axis=0).astype(BF16)
    c2, sn2 = _trig(i2, i2, s2)
    m2 = jnp.concatenate([c2, sn2], axis=1).astype(BF16)
    tw_c, tw_s = _trig(i1, i2, s1 * s2)
    return wc, m1, m2, tw_c[..., None], tw_s[..., None]


def _ffn_kernel(idx_cur, idx_nxt, gate_ref, x_hbm, w1_ref, w3_ref, w2_ref, ye_ref, xbuf, sem, *, tm, ffc):
    s = pl.program_id(0)
    ns = pl.num_programs(0)
    slot = s % 2

    def gather(idx_ref, sl):
        for r in range(tm):
            pltpu.make_async_copy(x_hbm.at[pl.ds(idx_ref[0, 0, r], 1), :], xbuf.at[sl, pl.ds(r, 1), :],
                                  sem.at[sl]).start()

    @pl.when(s == 0)
    def _():
        gather(idx_cur, 0)

    @pl.when((s + 1 < ns) & (slot == 0))
    def _():
        gather(idx_nxt, 1)

    @pl.when((s + 1 < ns) & (slot == 1))
    def _():
        gather(idx_nxt, 0)

    pltpu.make_async_copy(x_hbm.at[pl.ds(0, tm), :], xbuf.at[slot], sem.at[slot]).wait()
    xb = xbuf[slot].astype(BF16)
    dff = w1_ref.shape[1]
    acc = jnp.zeros(ye_ref.shape, F32)
    for c0 in range(0, dff, ffc):
        h1 = jnp.dot(xb, w1_ref[:, c0:c0 + ffc], preferred_element_type=F32)
        h3 = jnp.dot(xb, w3_ref[:, c0:c0 + ffc], preferred_element_type=F32)
        gg = (jax.nn.silu(h1) * h3).astype(BF16)
        acc = acc + jnp.dot(gg, w2_ref[c0:c0 + ffc, :], preferred_element_type=F32)
    ye_ref[...] = acc * gate_ref[...]


def _moe_ffn(x1, idx, gate, w1, w3, w2, layer):
    n, d = x1.shape
    e, cap = idx.shape
    dff = w1.shape[3]
    tm = min(256, cap)
    nblk = cap // tm
    ns = e * nblk
    ffc = dff // 2 if (dff // 2) % V7X_LANES == 0 else dff
    idx3 = idx.reshape(ns, 1, tm)
    gate2 = gate.reshape(e * cap, 1)
    smem_blk = lambda f: pl.BlockSpec((1, 1, tm), f, memory_space=pltpu.SMEM)
    return pl.pallas_call(
        functools.partial(_ffn_kernel, tm=tm, ffc=ffc),
        grid=(ns,),
        in_specs=[
            smem_blk(lambda s: (s, 0, 0)),
            smem_blk(lambda s: (jnp.minimum(s + 1, ns - 1), 0, 0)),
            pl.BlockSpec((tm, 1), lambda s: (s, 0)),
            pl.BlockSpec(memory_space=pl.ANY),
            pl.BlockSpec((None, None, d, dff), lambda s: (layer, s // nblk, 0, 0)),
            pl.BlockSpec((None, None, d, dff), lambda s: (layer, s // nblk, 0, 0)),
            pl.BlockSpec((None, None, dff, d), lambda s: (layer, s // nblk, 0, 0)),
        ],
        out_specs=pl.BlockSpec((tm, d), lambda s: (s, 0)),
        out_shape=jax.ShapeDtypeStruct((e * cap, d), F32),
        scratch_shapes=[pltpu.VMEM((2, tm, d), F32), pltpu.SemaphoreType.DMA((2,))],
        compiler_params=_cparams(("arbitrary",), 56),
        name="moe_ffn",
    )(idx3, idx3, gate2, x1, w1, w3, w2)


def _combine_kernel(tile_ref, chunk_ref, first_ref, last_ref, valid_ref,
                    slot_cur, slot_nxt, tok_ref, x_ref, ye_hbm, lg_ref, lb_ref, o_ref, acc_ref, buf, sem, *, tt, ca):
    w = pl.program_id(0)
    nw = pl.num_programs(0)
    par = w % 2

    def gather(slot_ref, sl):
        for r in range(ca):
            pltpu.make_async_copy(ye_hbm.at[pl.ds(slot_ref[0, 0, r], 1), :], buf.at[sl, pl.ds(r, 1), :],
                                  sem.at[sl]).start(priority=r % 2)

    @pl.when((w == 0) & (valid_ref[0] == 1))
    def _():
        gather(slot_cur, 0)

    nxt_valid = valid_ref[jnp.minimum(w + 1, nw - 1)] * (w + 1 < nw).astype(jnp.int32) == 1

    @pl.when(nxt_valid & (par == 0))
    def _():
        gather(slot_nxt, 1)

    @pl.when(nxt_valid & (par == 1))
    def _():
        gather(slot_nxt, 0)

    @pl.when(first_ref[w] == 1)
    def _():
        acc_ref[...] = jnp.zeros_like(acc_ref)

    @pl.when(valid_ref[w] == 1)
    def _():
        pltpu.make_async_copy(ye_hbm.at[pl.ds(0, ca), :], buf.at[par], sem.at[par]).wait()
        t0 = tile_ref[w] * tt
        row = lax.broadcasted_iota(jnp.int32, (tt, ca), 0) + t0
        onehot = jnp.where(tok_ref[0] == row, 1.0, 0.0).astype(BF16)
        acc_ref[...] += jnp.dot(onehot, buf[par].astype(BF16), preferred_element_type=F32)

    @pl.when(last_ref[w] == 1)
    def _():
        o_ref[...] = _layer_norm(DEEPNORM_ALPHA * x_ref[...] + acc_ref[...], lg_ref[...], lb_ref[...])


def _moe_combine(x1, ye, idx, lg, lb):
    n, d = x1.shape
    na = ye.shape[0]
    tt = min(256, n)
    ca = min(256, na)
    nt = n // tt
    nch = na // ca
    nw = nt + nch
    tok_flat = idx.reshape(-1)
    tok_sorted, order = lax.sort((tok_flat, jnp.arange(na, dtype=jnp.int32)), dimension=0, num_keys=1,
                                 is_stable=False)
    bounds = jnp.searchsorted(tok_sorted, jnp.arange(nt + 1, dtype=jnp.int32) * tt, side='left').astype(jnp.int32)
    start, end = bounds[:-1], bounds[1:]
    c_lo = jnp.minimum(start // ca, nch - 1)
    c_hi = jnp.where(end > start, (end - 1) // ca, c_lo)
    cnt = c_hi - c_lo + 1
    off = jnp.cumsum(cnt) - cnt
    total = off[-1] + cnt[-1]
    wi = jnp.arange(nw, dtype=jnp.int32)
    tile_w = jnp.clip(jnp.searchsorted(off, wi, side='right').astype(jnp.int32) - 1, 0, nt - 1)
    valid = wi < total
    rel = wi - off[tile_w]
    chunk_w = jnp.where(valid, c_lo[tile_w] + rel, nch - 1).astype(jnp.int32)
    first = (valid & (rel == 0)).astype(jnp.int32)
    last = (valid & (rel == cnt[tile_w] - 1)).astype(jnp.int32)
    tile_w = jnp.where(valid, tile_w, nt - 1).astype(jnp.int32)
    valid = valid.astype(jnp.int32)
    slot3 = order.reshape(nch, 1, ca)
    tok3 = tok_sorted.reshape(nch, 1, ca)
    grid_spec = pltpu.PrefetchScalarGridSpec(
        num_scalar_prefetch=5,
        grid=(nw,),
        in_specs=[
            pl.BlockSpec((1, 1, ca), lambda w, tl, ch, fi, la, va: (ch[w], 0, 0), memory_space=pltpu.SMEM),
            pl.BlockSpec((1, 1, ca), lambda w, tl, ch, fi, la, va: (ch[jnp.minimum(w + 1, nw - 1)], 0, 0),
                         memory_space=pltpu.SMEM),
            pl.BlockSpec((1, 1, ca), lambda w, tl, ch, fi, la, va: (ch[w], 0, 0)),
            pl.BlockSpec((tt, d), lambda w, tl, ch, fi, la, va: (tl[w], 0)),
            pl.BlockSpec(memory_space=pl.ANY),
            pl.BlockSpec((1, d), lambda w, tl, ch, fi, la, va: (0, 0)),
            pl.BlockSpec((1, d), lambda w, tl, ch, fi, la, va: (0, 0)),
        ],
        out_specs=pl.BlockSpec((tt, d), lambda w, tl, ch, fi, la, va: (tl[w], 0)),
        scratch_shapes=[pltpu.VMEM((tt, d), F32), pltpu.VMEM((2, ca, d), F32), pltpu.SemaphoreType.DMA((2,))],
    )
    return pl.pallas_call(
        functools.partial(_combine_kernel, tt=tt, ca=ca),
        grid_spec=grid_spec,
        out_shape=jax.ShapeDtypeStruct((n, d), F32),
        compiler_params=_cparams(("arbitrary",), 32),
        name="moe_combine",
    )(tile_w, chunk_w, first, last, valid, slot3, slot3, tok3, x1, ye, lg, lb)


def _expert_choice(x1, aff_t, w1, w3, w2, layer, lg, lb):
    ne, n = aff_t.shape
    cap = max(1, EC_CAPACITY_FACTOR * n // ne)
    gate, idx = lax.top_k(aff_t, cap)
    ye = _moe_ffn(x1, idx.astype(jnp.int32), gate, w1, w3, w2, layer)
    return _moe_combine(x1, ye, idx.astype(jnp.int32), lg, lb)


def _rope_tables(seq, reps):
    dim = ATT_HEAD_DIM
    inv = 1.0 / (ROPE_THETA ** (jnp.arange(0, dim, 2, dtype=F32) / dim))
    ang = jnp.arange(seq, dtype=F32)[:, None] * inv[None, :]
    ang = jnp.concatenate([ang, ang], -1)
    return jnp.tile(jnp.cos(ang), (1, reps)), jnp.tile(jnp.sin(ang), (1, reps))


def _even_layer(x3, p, s5tabs_fn, lambda_init, l):
    b, s, d = x3.shape
    n = b * s
    x2 = x3.reshape(n, d)
    qk = p['qk']
    av = p['av']
    cos_t, sin_t = _rope_tables(s, 2 * qk // ATT_HEAD_DIM)
    q, k, v, u = _proj_in(x2, p['w_in'], p['w_rot'], cos_t, sin_t, s, qk, av)
    heads = av // ATT_V_DIM
    tk = min(512, s // 2)
    vt = v.reshape(b, s // tk, tk, heads, ATT_V_DIM).transpose(0, 3, 1, 4, 2)
    ot = _diff_attention(q.reshape(b, s, qk), k.reshape(b, s, qk), vt, p['lam'], p['subln_g'], lambda_init, tk)
    attn = ot.transpose(0, 3, 1, 2).reshape(n, av)
    L = min(S5_CHUNK, s)
    nc = s // L
    G, H = p['s5_g'], p['s5_h']
    ug = u.reshape(b, nc, L, G, H).transpose(3, 0, 1, 2, 4).reshape(G, b * nc, L * H)
    yg = _s5_scan(ug, s5tabs_fn(L, nc), nc)
    yg = yg.reshape(G, b, nc, L, H).transpose(1, 2, 3, 0, 4).reshape(n, G * H)
    return _mix_even(x2, attn, yg, p['glu_w'], p['glu_b'], p['w_out_even'], p['ln_mix_g'][l], p['ln_mix_b'][l],
                     p['w_router'][l])


def _odd_layer(x3, p, l):
    b, s, d = x3.shape
    gw = d // FNET_GROUPS
    s2 = 128 if s >= 1024 else s // 8
    s1 = s // s2
    wc, m1, m2, tw_c, tw_s = _dft_tables(s1, s2, gw)
    z = _chan_dft(x3, wc).reshape(b, 2 * s1, s2 * d)
    g = _dft_stage1(m1, z).reshape(b, 2, s1, s2, d)
    f = _dft_stage2(m2, g, tw_c, tw_s, 1.0 / math.sqrt(s * gw))
    return _mix_odd(x3.reshape(b * s, d), f.reshape(b * s, d), p['w_out_odd'], p['ln_mix_g'][l], p['ln_mix_b'][l],
                    p['w_router'][l])


def _trunk(x3, p, s5tabs_fn):
    b, s, d = x3.shape
    for l in range(DEPTH):
        if l % 2 == 0:
            lambda_init = 0.8 - 0.6 * math.exp(-0.3 * l)
            x1, aff = _even_layer(x3, p, s5tabs_fn, lambda_init, l)
        else:
            x1, aff = _odd_layer(x3, p, l)
        x2 = _expert_choice(x1, aff, p['w_ff1'], p['w_ff3'], p['w_ff2'], l, p['ln_ffn_g'][l], p['ln_ffn_b'][l])
        x3 = x2.reshape(b, s, d)
    return x3


def kernel(x_prompt, x_sample, w_in, lam_q1, lam_k1, lam_q2, lam_k2, subln_g, s5_a_re, s5_a_im, s5_log_dt, s5_b_re, s5_b_im, s5_c_re, s5_c_im, s5_d, s5_glu_w, s5_glu_b, w_out_even, w_out_odd, ln_mix_g, ln_mix_b, w_router, w_ff1, w_ff3, w_ff2, ln_ffn_g, ln_ffn_b):
    assert w_in.shape[0] == 1 and w_out_odd.shape[0] == 1 and DEPTH == 2
    d = w_in.shape[1]
    av = 4 * ATT_V_DIM
    qk = av
    wi = w_in[0]
    wqk = wi[:, :2 * qk].reshape(d, 2 * qk // ATT_HEAD_DIM, 2, ATT_HEAD_DIM // 2)
    w_rot = jnp.stack([-wqk[:, :, 1], wqk[:, :, 0]], axis=2).reshape(d, 2 * qk)
    lam = (jnp.exp(jnp.sum(lam_q1[0] * lam_k1[0])) - jnp.exp(jnp.sum(lam_q2[0] * lam_k2[0]))
           + (0.8 - 0.6 * math.exp(0.0))).reshape(1).astype(F32)
    p = {
        'qk': qk, 'av': av, 's5_g': s5_b_re.shape[2], 's5_h': s5_b_re.shape[4],
        'w_in': wi.astype(BF16), 'w_rot': w_rot.astype(BF16), 'lam': lam,
        'subln_g': subln_g[0].reshape(-1, 1),
        'glu_w': s5_glu_w[0].astype(BF16), 'glu_b': s5_glu_b[0].reshape(1, -1),
        'w_out_even': w_out_even[0].astype(BF16), 'w_out_odd': w_out_odd[0].astype(BF16),
        'ln_mix_g': ln_mix_g[:, None, :], 'ln_mix_b': ln_mix_b[:, None, :],
        'ln_ffn_g': ln_ffn_g[:, None, :], 'ln_ffn_b': ln_ffn_b[:, None, :],
        'w_router': w_router.transpose(0, 2, 1),
        'w_ff1': w_ff1.astype(BF16), 'w_ff3': w_ff3.astype(BF16), 'w_ff2': w_ff2.astype(BF16),
    }

    def s5tabs_fn(chunk, nc):
        nsteps = max(1, (nc - 1).bit_length())
        return _s5_tables(s5_a_re[0], s5_a_im[0], s5_log_dt[0], s5_b_re[0], s5_b_im[0], s5_c_re[0], s5_c_im[0],
                          s5_d[0], chunk, nsteps)

    return (_trunk(x_prompt, p, s5tabs_fn), _trunk(x_sample, p, s5tabs_fn))
```
